```python
import math
import functools
import jax
import jax.numpy as jnp
from jax import lax
import numpy as np

D_MODEL = 1024
BATCH = 8
SEQ = 2048
DEPTH = 1
DEC_BATCH = 128
DEC_SEQ = 4
PAST_LEN = 8192
PAGE_SIZE = 128

N_DIFF_HEADS = 4
DIFF_DH = 64
DIFF_VDIM = 2 * DIFF_DH
DIFF_ROT = DIFF_DH // 4
N_MLA_HEADS = 8
MLA_NOPE = 64
MLA_ROPE = 32
MLA_V = 64
MLA_Q_RANK = 384
MLA_KV_RANK = 256
DIFF_WIDTH = N_DIFF_HEADS * DIFF_VDIM
MLA_WIDTH = N_MLA_HEADS * MLA_V
MIX_WIDTH = DIFF_WIDTH + MLA_WIDTH
DQ_COLS = N_DIFF_HEADS * 2 * DIFF_DH
DK_COLS = N_DIFF_HEADS * 2 * DIFF_DH
DV_COLS = N_DIFF_HEADS * DIFF_VDIM
IN_COLS = DQ_COLS + DK_COLS + DV_COLS + MLA_Q_RANK + MLA_KV_RANK + MLA_ROPE
IN_SPLIT = (DQ_COLS, DQ_COLS + DK_COLS, DQ_COLS + DK_COLS + DV_COLS,
            DQ_COLS + DK_COLS + DV_COLS + MLA_Q_RANK,
            DQ_COLS + DK_COLS + DV_COLS + MLA_Q_RANK + MLA_KV_RANK)
D_FF = ((8 * D_MODEL + 3 * 256 - 1) // (3 * 256)) * 256
ROPE_THETA = 500000.0
NORM_EPS = 1e-6
Q_BLOCK = 128
DIFF_SCALE = DIFF_DH ** -0.5
MLA_SCALE = (MLA_NOPE + MLA_ROPE) ** -0.5

kernel_name = 'hybrid_diffattn_mla_adaln_decoder_step'

F32 = jnp.float32


def rms_norm(x, g):
    xf = x.astype(F32)
    y = xf * lax.rsqrt(jnp.mean(xf * xf, axis=-1, keepdims=True) + NORM_EPS)
    return (y * g.astype(F32)).astype(x.dtype)


def rope(x, pos):
    d = x.shape[-1]
    half = d // 2
    inv = ROPE_THETA ** (-(jnp.arange(half, dtype=F32) * 2.0 / d))
    ang = pos.astype(F32)[:, None] * inv[None, :]
    shp = (pos.shape[0],) + (1,) * (x.ndim - 3) + (half,)
    cos = jnp.cos(ang).reshape(shp)
    sin = jnp.sin(ang).reshape(shp)
    xf = x.astype(F32)
    x1, x2 = xf[..., :half], xf[..., half:]
    return jnp.concatenate([x1 * cos - x2 * sin, x2 * cos + x1 * sin], axis=-1).astype(x.dtype)


def partial_rope(x, pos):
    return jnp.concatenate([rope(x[..., :DIFF_ROT], pos), x[..., DIFF_ROT:]], axis=-1)


def lambda_init_fn(layer):
    return 0.8 - 0.6 * math.exp(-0.3 * layer)


def modulation(c, mod_w, mod_b):
    m = jax.nn.silu(c) @ mod_w + mod_b
    return jnp.split(m[:, None, :], 6, axis=-1)


def mix_inputs(h, pos, w_in, q_norm, kv_norm, w_uq, w_ukv):
    B, T, _ = h.shape
    z = h @ w_in
    dq, dk, dv, cq, ckv, kpe = jnp.split(z, IN_SPLIT, axis=-1)
    dq = partial_rope(dq.reshape(B, T, N_DIFF_HEADS, 2, DIFF_DH), pos)
    dk = partial_rope(dk.reshape(B, T, N_DIFF_HEADS, 2, DIFF_DH), pos)
    dv = dv.reshape(B, T, N_DIFF_HEADS, DIFF_VDIM)
    q = (rms_norm(cq, q_norm) @ w_uq).reshape(B, T, N_MLA_HEADS, MLA_NOPE + MLA_ROPE)
    q_nope = q[..., :MLA_NOPE]
    q_pe = rope(q[..., MLA_NOPE:], pos)
    ckv = rms_norm(ckv, kv_norm)
    kpe = rope(kpe, pos)
    w_uk = w_ukv.reshape(MLA_KV_RANK, N_MLA_HEADS, MLA_NOPE + MLA_V)[..., :MLA_NOPE]
    q_lat = jnp.einsum('bthn,rhn->bthr', q_nope, w_uk)
    return dq, dk, dv, q_lat, q_pe, ckv, kpe


def diff_scores(dq, dk):
    return jnp.einsum('bqhmd,bkhmd->bhmqk', dq, dk, preferred_element_type=F32) * DIFF_SCALE


def mla_scores(q_lat, q_pe, ckv, kpe):
    s = jnp.einsum('bqhr,bkr->bhqk', q_lat, ckv, preferred_element_type=F32)
    s = s + jnp.einsum('bqhp,bkp->bhqk', q_pe, kpe, preferred_element_type=F32)
    return s * MLA_SCALE


def prompt_attention(dq, dk, dv, q_lat, q_pe, ckv, kpe, lam):
    B, T = dq.shape[:2]
    n_blocks = T // Q_BLOCK
    kpos = jnp.arange(T)
    dvf = dv.astype(F32)
    ckvf = ckv.astype(F32)

    def block(i):
        s0 = i * Q_BLOCK
        qpos = s0 + jnp.arange(Q_BLOCK)
        mask = kpos[None, :] <= qpos[:, None]
        sl = lambda a: lax.dynamic_slice_in_dim(a, s0, Q_BLOCK, axis=1)
        pd = jax.nn.softmax(jnp.where(mask, diff_scores(sl(dq), dk), -jnp.inf), axis=-1)
        od = jnp.einsum('bhmqk,bkhv->bhmqv', pd, dvf)
        od = od[:, :, 0] - lam * od[:, :, 1]
        pm = jax.nn.softmax(jnp.where(mask, mla_scores(sl(q_lat), sl(q_pe), ckv, kpe), -jnp.inf), axis=-1)
        om = jnp.einsum('bhqk,bkr->bhqr', pm, ckvf)
        return od, om

    od, om = lax.map(block, jnp.arange(n_blocks))
    od = od.transpose(1, 0, 3, 2, 4).reshape(B, T, N_DIFF_HEADS, DIFF_VDIM)
    om = om.transpose(1, 0, 3, 2, 4).reshape(B, T, N_MLA_HEADS, MLA_KV_RANK)
    return od, om


def online_update(m, l, acc, s, pv_fn):
    m_new = jnp.maximum(m, s.max(axis=-1))
    alpha = jnp.exp(m - m_new)
    p = jnp.exp(s - m_new[..., None])
    return m_new, l * alpha + p.sum(axis=-1), acc * alpha[..., None] + pv_fn(p)


def sample_attention(dq, dk, dv, q_lat, q_pe, ckv, kpe, lam,
                     cache_k_diff, cache_v_diff, cache_ckv, cache_kpe, page_table, layer):
    B, Q = dq.shape[:2]
    init = (jnp.full((B, N_DIFF_HEADS, 2, Q), -jnp.inf, F32),
            jnp.zeros((B, N_DIFF_HEADS, 2, Q), F32),
            jnp.zeros((B, N_DIFF_HEADS, 2, Q, DIFF_VDIM), F32),
            jnp.full((B, N_MLA_HEADS, Q), -jnp.inf, F32),
            jnp.zeros((B, N_MLA_HEADS, Q), F32),
            jnp.zeros((B, N_MLA_HEADS, Q, MLA_KV_RANK), F32))

    def attend(carry, kd, vd, lat, pe, mask):
        md, ld, ad, mm, lm, am = carry
        sd = diff_scores(dq, kd)
        sm = mla_scores(q_lat, q_pe, lat, pe)
        if mask is not None:
            sd = jnp.where(mask, sd, -jnp.inf)
            sm = jnp.where(mask, sm, -jnp.inf)
        vdf = vd.astype(F32)
        latf = lat.astype(F32)
        md, ld, ad = online_update(md, ld, ad, sd, lambda p: jnp.einsum('bhmqk,bkhv->bhmqv', p, vdf))
        mm, lm, am = online_update(mm, lm, am, sm, lambda p: jnp.einsum('bhqk,bkr->bhqr', p, latf))
        return (md, ld, ad, mm, lm, am)

    def step(carry, phys):
        carry = attend(carry, cache_k_diff[layer, phys], cache_v_diff[layer, phys],
                       cache_ckv[layer, phys], cache_kpe[layer, phys], None)
        return carry, None

    carry, _ = lax.scan(step, init, page_table.T)
    self_mask = jnp.arange(Q)[None, :] <= jnp.arange(Q)[:, None]
    md, ld, ad, mm, lm, am = attend(carry, dk, dv, ckv, kpe, self_mask)
    od = ad / ld[..., None]
    od = (od[:, :, 0] - lam * od[:, :, 1]).transpose(0, 2, 1, 3)
    om = (am / lm[..., None]).transpose(0, 2, 1, 3)
    return od, om


def mix_output(od, om, diff_subln, w_ukv, w_o, lam_init):
    B, T = od.shape[:2]
    od = (rms_norm(od, diff_subln) * (1.0 - lam_init)).reshape(B, T, DIFF_WIDTH)
    w_uv = w_ukv.reshape(MLA_KV_RANK, N_MLA_HEADS, MLA_NOPE + MLA_V)[..., MLA_NOPE:]
    om = jnp.einsum('bthr,rhv->bthv', om, w_uv).reshape(B, T, MLA_WIDTH)
    return jnp.concatenate([od, om], axis=-1).astype(w_o.dtype) @ w_o


def decoder_layer(x, c, pos, attend, lam, lam_init, mod_w, mod_b, attn_norm, w_in, q_norm,
                  kv_norm, w_uq, w_ukv, diff_subln, w_o, ffn_norm, w_gate_up, w_down):
    sh_a, sc_a, g_a, sh_f, sc_f, g_f = modulation(c, mod_w, mod_b)
    h = rms_norm(x, attn_norm) * (1.0 + sc_a) + sh_a
    dq, dk, dv, q_lat, q_pe, ckv, kpe = mix_inputs(h, pos, w_in, q_norm, kv_norm, w_uq, w_ukv)
    od, om = attend(dq, dk, dv, q_lat, q_pe, ckv, kpe, lam)
    x = x + g_a * mix_output(od, om, diff_subln, w_ukv, w_o, lam_init).astype(x.dtype)
    h = rms_norm(x, ffn_norm) * (1.0 + sc_f) + sh_f
    gate, up = jnp.split(h @ w_gate_up, 2, axis=-1)
    x = x + g_f * ((jax.nn.silu(gate) * up) @ w_down)
    return x, (dk, dv, ckv, kpe)


def setup_inputs(seed: int = 0) -> dict:
    key = jax.random.key(seed)
    ks = jax.random.split(key, 32)
    n_pages = PAST_LEN // PAGE_SIZE
    n_pool = (DEC_BATCH * n_pages * 5) // 4

    def nrm(k, shape, scale):
        return jax.random.normal(k, shape, F32) * scale

    def gain(k, shape):
        return 1.0 + 0.05 * jax.random.normal(k, shape, F32)

    page_table = jax.random.permutation(ks[8], n_pool)[: DEC_BATCH * n_pages]
    page_table = page_table.reshape(DEC_BATCH, n_pages).astype(jnp.int32)
    return {
        'x_prompt': jax.random.normal(ks[0], (BATCH, SEQ, D_MODEL), F32),
        'x_sample': jax.random.normal(ks[1], (DEC_BATCH, DEC_SEQ, D_MODEL), F32),
        'c_prompt': jax.random.normal(ks[2], (BATCH, D_MODEL), F32),
        'c_sample': jax.random.normal(ks[3], (DEC_BATCH, D_MODEL), F32),
        'cache_k_diff': jax.random.normal(ks[4], (DEPTH, n_pool, PAGE_SIZE, N_DIFF_HEADS, 2, DIFF_DH), F32),
        'cache_v_diff': jax.random.normal(ks[5], (DEPTH, n_pool, PAGE_SIZE, N_DIFF_HEADS, DIFF_VDIM), F32),
        'cache_ckv': jax.random.normal(ks[6], (DEPTH, n_pool, PAGE_SIZE, MLA_KV_RANK), F32),
        'cache_kpe': jax.random.normal(ks[7], (DEPTH, n_pool, PAGE_SIZE, MLA_ROPE), F32),
        'page_table': page_table,
        'mod_w': nrm(ks[9], (DEPTH, D_MODEL, 6 * D_MODEL), D_MODEL ** -0.5),
        'mod_b': nrm(ks[10], (DEPTH, 6 * D_MODEL), 0.01),
        'attn_norm': gain(ks[11], (DEPTH, D_MODEL)),
        'w_in': nrm(ks[12], (DEPTH, D_MODEL, IN_COLS), D_MODEL ** -0.5),
        'q_norm': gain(ks[13], (DEPTH, MLA_Q_RANK)),
        'kv_norm': gain(ks[14], (DEPTH, MLA_KV_RANK)),
        'w_uq': nrm(ks[15], (DEPTH, MLA_Q_RANK, N_MLA_HEADS * (MLA_NOPE + MLA_ROPE)), MLA_Q_RANK ** -0.5),
        'w_ukv': nrm(ks[16], (DEPTH, MLA_KV_RANK, N_MLA_HEADS * (MLA_NOPE + MLA_V)), MLA_KV_RANK ** -0.5),
        'lambda_q1': nrm(ks[17], (DEPTH, DIFF_DH), 0.1),
        'lambda_k1': nrm(ks[18], (DEPTH, DIFF_DH), 0.1),
        'lambda_q2': nrm(ks[19], (DEPTH, DIFF_DH), 0.1),
        'lambda_k2': nrm(ks[20], (DEPTH, DIFF_DH), 0.1),
        'diff_subln': gain(ks[21], (DEPTH, DIFF_VDIM)),
        'w_o': nrm(ks[22], (DEPTH, MIX_WIDTH, D_MODEL), MIX_WIDTH ** -0.5),
        'ffn_norm': gain(ks[23], (DEPTH, D_MODEL)),
        'w_gate_up': nrm(ks[24], (DEPTH, D_MODEL, 2 * D_FF), D_MODEL ** -0.5),
        'w_down': nrm(ks[25], (DEPTH, D_FF, D_MODEL), D_FF ** -0.5),
        'final_norm': gain(ks[26], (D_MODEL,)),
    }


def reference(x_prompt, x_sample, c_prompt, c_sample, cache_k_diff, cache_v_diff, cache_ckv,
              cache_kpe, page_table, mod_w, mod_b, attn_norm, w_in, q_norm, kv_norm, w_uq, w_ukv,
              lambda_q1, lambda_k1, lambda_q2, lambda_k2, diff_subln, w_o, ffn_norm, w_gate_up,
              w_down, final_norm):
    pos_p = jnp.arange(x_prompt.shape[1])
    pos_s = PAST_LEN + jnp.arange(x_sample.shape[1])
    xp, xs = x_prompt, x_sample
    states_p, states_s = [], []
    for l in range(DEPTH):
        lam_init = lambda_init_fn(l)
        lam = (jnp.exp(jnp.sum(lambda_q1[l].astype(F32) * lambda_k1[l].astype(F32)))
               - jnp.exp(jnp.sum(lambda_q2[l].astype(F32) * lambda_k2[l].astype(F32))) + lam_init)
        weights = (mod_w[l], mod_b[l], attn_norm[l], w_in[l], q_norm[l], kv_norm[l], w_uq[l],
                   w_ukv[l], diff_subln[l], w_o[l], ffn_norm[l], w_gate_up[l], w_down[l])
        xp, st_p = decoder_layer(xp, c_prompt, pos_p, prompt_attention, lam, lam_init, *weights)
        attend_s = functools.partial(sample_attention, cache_k_diff=cache_k_diff,
                                     cache_v_diff=cache_v_diff, cache_ckv=cache_ckv,
                                     cache_kpe=cache_kpe, page_table=page_table, layer=l)
        xs, st_s = decoder_layer(xs, c_sample, pos_s, attend_s, lam, lam_init, *weights)
        states_p.append(st_p)
        states_s.append(st_s)
    y_prompt = rms_norm(xp, final_norm)
    y_sample = rms_norm(xs, final_norm)
    new_k_diff_prompt = jnp.stack([s[0] for s in states_p])
    new_v_diff_prompt = jnp.stack([s[1] for s in states_p])
    new_ckv_prompt = jnp.stack([s[2] for s in states_p])
    new_kpe_prompt = jnp.stack([s[3] for s in states_p])
    new_k_diff_sample = jnp.stack([s[0] for s in states_s])
    new_v_diff_sample = jnp.stack([s[1] for s in states_s])
    new_ckv_sample = jnp.stack([s[2] for s in states_s])
    new_kpe_sample = jnp.stack([s[3] for s in states_s])
    return (y_prompt, y_sample, new_k_diff_prompt, new_v_diff_prompt, new_ckv_prompt,
            new_kpe_prompt, new_k_diff_sample, new_v_diff_sample, new_ckv_sample, new_kpe_sample)
```

```python
import functools

import jax
import jax.numpy as jnp
from jax import lax
from jax.experimental import pallas as pl
from jax.experimental.pallas import tpu as pltpu

F32 = jnp.float32
BF16 = jnp.bfloat16

D_MODEL = 1024
N_DIFF_HEADS = 4
DIFF_DH = 64
DIFF_VDIM = 2 * DIFF_DH
DIFF_ROT = DIFF_DH // 4
N_MLA_HEADS = 8
MLA_NOPE = 64
MLA_ROPE = 32
MLA_V = 64
MLA_Q_RANK = 384
MLA_KV_RANK = 256
DIFF_WIDTH = N_DIFF_HEADS * DIFF_VDIM
MLA_WIDTH = N_MLA_HEADS * MLA_V
DQ_COLS = N_DIFF_HEADS * 2 * DIFF_DH
IN_COLS = 3 * DQ_COLS + MLA_Q_RANK + MLA_KV_RANK + MLA_ROPE
D_FF = 2816
ROPE_THETA = 500000.0
NORM_EPS = 1e-6
PAGE_SIZE = 128
DIFF_SCALE = DIFF_DH ** -0.5
MLA_SCALE = (MLA_NOPE + MLA_ROPE) ** -0.5
LAMBDA_INIT = 0.8 - 0.6

LANES = 128
IN_COLS_PAD = 2304
QCAT = MLA_KV_RANK + LANES
PAGES_PER_CHUNK = 8
VMEM_LIMIT = 56 * 1024 * 1024


def _nt_dot(a, b):
    return lax.dot_general(a, b, (((1,), (1,)), ((), ())), preferred_element_type=F32)


def _dot(a, b):
    return jnp.dot(a, b, preferred_element_type=F32)


def _rms(x, g):
    return x * lax.rsqrt(jnp.mean(x * x, axis=-1, keepdims=True) + NORM_EPS) * g


def _rope(x, tab_ref, half):
    return (x * tab_ref[0]
            + pltpu.roll(x, LANES - half, 1) * tab_ref[1]
            + pltpu.roll(x, half, 1) * tab_ref[2])


def _const_spec(shape):
    nd = len(shape)
    return pl.BlockSpec(shape, lambda *_: (0,) * nd, pipeline_mode=pl.Buffered(1))


def _mod_kernel(c_ref, w_ref, b_ref, lv_ref, m_ref, lam_ref):
    c = c_ref[...]
    a = (c * jax.nn.sigmoid(c)).astype(BF16)
    m_ref[...] = _dot(a, w_ref[...].astype(BF16)) + b_ref[...]
    lv = lv_ref[...]
    s1 = jnp.sum(lv[0:1] * lv[1:2], axis=-1, keepdims=True)
    s2 = jnp.sum(lv[2:3] * lv[3:4], axis=-1, keepdims=True)
    lam_ref[...] = jnp.broadcast_to(jnp.exp(s1) - jnp.exp(s2) + LAMBDA_INIT, lam_ref.shape)


def _modulation(c_all, mod_w, mod_b, lam_vecs):
    n = c_all.shape[0]
    tn = 1024
    return pl.pallas_call(
        _mod_kernel,
        grid=(6 * D_MODEL // tn,),
        in_specs=[pl.BlockSpec((n, D_MODEL), lambda j: (0, 0)),
                  pl.BlockSpec((D_MODEL, tn), lambda j: (0, j)),
                  pl.BlockSpec((1, tn), lambda j: (0, j)),
                  pl.BlockSpec((4, DIFF_DH), lambda j: (0, 0))],
        out_specs=[pl.BlockSpec((n, tn), lambda j: (0, j)),
                   pl.BlockSpec((1, LANES), lambda j: (0, 0))],
        out_shape=[jax.ShapeDtypeStruct((n, 6 * D_MODEL), F32),
                   jax.ShapeDtypeStruct((1, LANES), F32)],
        compiler_params=pltpu.CompilerParams(dimension_semantics=("arbitrary",)),
        name="modulation",
    )(c_all, mod_w, mod_b, lam_vecs)


def _in_kernel(x_ref, sh_ref, sc_ref, rd_ref, rp_ref, an_ref, win_ref, qn_ref, kn_ref, wuq_ref, wuk_ref,
               dq_ref, dkf_ref, dkb_ref, dvf_ref, dvb_ref, ckvf_ref, kpef_ref, kcat_ref, qcat_ref):
    x = x_ref[0]
    tm = x.shape[0]
    h = (_rms(x, an_ref[...]) * (1.0 + sc_ref[0]) + sh_ref[0]).astype(BF16)
    z = _dot(h, win_ref[...])
    for j in range(DQ_COLS // LANES):
        sl = slice(LANES * j, LANES * (j + 1))
        dq_ref[0, :, sl] = (_rope(z[:, sl], rd_ref, DIFF_ROT // 2) * DIFF_SCALE).astype(BF16)
        k = _rope(z[:, DQ_COLS + LANES * j:DQ_COLS + LANES * (j + 1)], rd_ref, DIFF_ROT // 2)
        dkf_ref[0, :, sl] = k
        dkb_ref[0, :, sl] = k.astype(BF16)
    dv = z[:, 2 * DQ_COLS:3 * DQ_COLS]
    dvf_ref[0] = dv
    dvb_ref[0] = dv.astype(BF16)

    c0 = 3 * DQ_COLS
    cqn = _rms(z[:, c0:c0 + MLA_Q_RANK], qn_ref[...]).astype(BF16)
    q = _dot(cqn, wuq_ref[...])
    nope_w = N_MLA_HEADS * MLA_NOPE
    qn = q[:, :nope_w].astype(BF16)
    for j in range(N_MLA_HEADS // 2):
        ql = _dot(qn[:, LANES * j:LANES * (j + 1)], wuk_ref[j])
        qcat_ref[0, 2 * j, :, 0:MLA_KV_RANK] = (ql[:, :MLA_KV_RANK] * MLA_SCALE).astype(BF16)
        qcat_ref[0, 2 * j + 1, :, 0:MLA_KV_RANK] = (ql[:, MLA_KV_RANK:] * MLA_SCALE).astype(BF16)
    lane = lax.broadcasted_iota(jnp.int32, (tm, LANES), 1)
    heads_per_slab = LANES // MLA_ROPE
    for g in range(N_MLA_HEADS // heads_per_slab):
        qp = _rope(q[:, nope_w + LANES * g:nope_w + LANES * (g + 1)], rp_ref, MLA_ROPE // 2) * MLA_SCALE
        for i in range(heads_per_slab):
            moved = qp if i == 0 else pltpu.roll(qp, LANES - MLA_ROPE * i, 1)
            qcat_ref[0, heads_per_slab * g + i, :, MLA_KV_RANK:QCAT] = (
                jnp.where(lane < MLA_ROPE, moved, 0.0).astype(BF16))

    c1 = c0 + MLA_Q_RANK
    ckvn = _rms(z[:, c1:c1 + MLA_KV_RANK], kn_ref[...])
    ckvf_ref[0] = ckvn
    kpe = _rope(z[:, c1 + MLA_KV_RANK:IN_COLS_PAD], rp_ref, MLA_ROPE // 2)
    kpef_ref[0] = kpe[:, :MLA_ROPE]
    kcat_ref[0, :, 0:MLA_KV_RANK] = ckvn.astype(BF16)
    kcat_ref[0, :, MLA_KV_RANK:QCAT] = kpe.astype(BF16)


def _in_proj(x, mod, rope_d, rope_p, attn_norm, w_in_p, q_norm, kv_norm, w_uq_p, w_uk_pairs, *, tm):
    nb, t, _ = x.shape
    r = mod.shape[1]
    rb = 1 if r == 1 else tm
    mod_idx = (lambda col: (lambda b, i: (b, 0, col))) if r == 1 else (lambda col: (lambda b, i: (b, i, col)))
    tok = lambda w: pl.BlockSpec((1, tm, w), lambda b, i: (b, i, 0))
    rope_spec = pl.BlockSpec((3, tm, LANES), lambda b, i: (0, i, 0))
    out_shape = [jax.ShapeDtypeStruct((nb, t, DQ_COLS), BF16),
                 jax.ShapeDtypeStruct((nb, t, DQ_COLS), F32), jax.ShapeDtypeStruct((nb, t, DQ_COLS), BF16),
                 jax.ShapeDtypeStruct((nb, t, DQ_COLS), F32), jax.ShapeDtypeStruct((nb, t, DQ_COLS), BF16),
                 jax.ShapeDtypeStruct((nb, t, MLA_KV_RANK), F32),
                 jax.ShapeDtypeStruct((nb, t, MLA_ROPE), F32),
                 jax.ShapeDtypeStruct((nb, t, QCAT), BF16),
                 jax.ShapeDtypeStruct((nb, N_MLA_HEADS, t, QCAT), BF16)]
    out_specs = [tok(DQ_COLS), tok(DQ_COLS), tok(DQ_COLS), tok(DQ_COLS), tok(DQ_COLS),
                 tok(MLA_KV_RANK), tok(MLA_ROPE), tok(QCAT),
                 pl.BlockSpec((1, N_MLA_HEADS, tm, QCAT), lambda b, i: (b, 0, i, 0))]
    return pl.pallas_call(
        _in_kernel,
        grid=(nb, t // tm),
        in_specs=[tok(D_MODEL),
                  pl.BlockSpec((1, rb, D_MODEL), mod_idx(0)),
                  pl.BlockSpec((1, rb, D_MODEL), mod_idx(1)),
                  rope_spec, rope_spec,
                  _const_spec((1, D_MODEL)),
                  _const_spec((D_MODEL, IN_COLS_PAD)),
                  _const_spec((1, MLA_Q_RANK)),
                  _const_spec((1, MLA_KV_RANK)),
                  _const_spec((MLA_Q_RANK, N_MLA_HEADS * (MLA_NOPE + MLA_ROPE))),
                  _const_spec((N_MLA_HEADS // 2, LANES, 2 * MLA_KV_RANK))],
        out_specs=out_specs,
        out_shape=out_shape,
        compiler_params=pltpu.CompilerParams(dimension_semantics=("parallel", "parallel"),
                                             vmem_limit_bytes=VMEM_LIMIT),
        name="in_proj",
    )(x, mod, mod, rope_d, rope_p, attn_norm, w_in_p, q_norm, kv_norm, w_uq_p, w_uk_pairs)


def _online_update(s, m_ref, l_ref, acc_ref, v, idx):
    m_old = m_ref[idx]
    m_new = jnp.maximum(m_old, jnp.max(s, axis=-1, keepdims=True))
    alpha = jnp.exp(m_old - m_new)
    p = jnp.exp(s - m_new)
    l_ref[idx] = alpha * l_ref[idx] + jnp.sum(p, axis=-1, keepdims=True)
    acc_ref[idx] = alpha * acc_ref[idx] + _dot(p.astype(BF16), v)
    m_ref[idx] = m_new


def _prompt_attn_kernel(lam_ref, dq_ref, qcat_ref, dk_ref, dv_ref, kcat_ref, od_ref, om_ref,
                        qd_s, md_s, ld_s, accd_s, mm_s, lm_s, accm_s, *, tq):
    qi = pl.program_id(1)
    lane = lax.broadcasted_iota(jnp.int32, (tq, LANES), 1)
    for h in range(N_DIFF_HEADS):
        qh = dq_ref[0, :, LANES * h:LANES * (h + 1)]
        qd_s[h, 0:tq, :] = jnp.where(lane < DIFF_DH, qh, jnp.zeros_like(qh))
        qd_s[h, tq:2 * tq, :] = jnp.where(lane >= DIFF_DH, qh, jnp.zeros_like(qh))
    md_s[...] = jnp.full(md_s.shape, -jnp.inf, F32)
    ld_s[...] = jnp.zeros(ld_s.shape, F32)
    accd_s[...] = jnp.zeros(accd_s.shape, F32)
    mm_s[...] = jnp.full(mm_s.shape, -jnp.inf, F32)
    lm_s[...] = jnp.zeros(lm_s.shape, F32)
    accm_s[...] = jnp.zeros(accm_s.shape, F32)
    qm = qcat_ref[0].reshape(N_MLA_HEADS * tq, QCAT)

    def body(kt, carry):
        k0 = pl.multiple_of(kt * tq, tq)
        past = kt < qi

        def allowed(rows):
            r = lax.broadcasted_iota(jnp.int32, (rows, tq), 0) % tq
            c = lax.broadcasted_iota(jnp.int32, (rows, tq), 1)
            return (c <= r) | past

        ok_d = allowed(2 * tq)
        for h in range(N_DIFF_HEADS):
            sl = slice(LANES * h, LANES * (h + 1))
            s = _nt_dot(qd_s[h], dk_ref[0, pl.ds(k0, tq), sl])
            s = jnp.where(ok_d, s, -jnp.inf)
            _online_update(s, md_s, ld_s, accd_s, dv_ref[0, pl.ds(k0, tq), sl], h)
        kc = kcat_ref[0, pl.ds(k0, tq), :]
        s = _nt_dot(qm, kc)
        s = jnp.where(allowed(N_MLA_HEADS * tq), s, -jnp.inf)
        _online_update(s, mm_s, lm_s, accm_s, kc[:, :MLA_KV_RANK], slice(None))
        return carry

    lax.fori_loop(0, qi + 1, body, 0)

    lam = lam_ref[...]
    for h in range(N_DIFF_HEADS):
        o = accd_s[h] / ld_s[h]
        od_ref[0, :, LANES * h:LANES * (h + 1)] = o[:tq] - lam * o[tq:]
    om = accm_s[...] / lm_s[...]
    for h in range(N_MLA_HEADS):
        om_ref[0, :, MLA_KV_RANK * h:MLA_KV_RANK * (h + 1)] = om[h * tq:(h + 1) * tq].astype(BF16)


def _prompt_attn(lam, dq, qcat, dk, dv, kcat, *, tq):
    b, t, _ = dq.shape
    kv = lambda w: pl.BlockSpec((1, t, w), lambda bi, qi: (bi, 0, 0))
    return pl.pallas_call(
        functools.partial(_prompt_attn_kernel, tq=tq),
        grid=(b, t // tq),
        in_specs=[pl.BlockSpec((1, LANES), lambda bi, qi: (0, 0)),
                  pl.BlockSpec((1, tq, DQ_COLS), lambda bi, qi: (bi, qi, 0)),
                  pl.BlockSpec((1, N_MLA_HEADS, tq, QCAT), lambda bi, qi: (bi, 0, qi, 0)),
                  kv(DQ_COLS), kv(DQ_COLS), kv(QCAT)],
        out_specs=[pl.BlockSpec((1, tq, DIFF_WIDTH), lambda bi, qi: (bi, qi, 0)),
                   pl.BlockSpec((1, tq, N_MLA_HEADS * MLA_KV_RANK), lambda bi, qi: (bi, qi, 0))],
        out_shape=[jax.ShapeDtypeStruct((b, t, DIFF_WIDTH), F32),
                   jax.ShapeDtypeStruct((b, t, N_MLA_HEADS * MLA_KV_RANK), BF16)],
        scratch_shapes=[pltpu.VMEM((N_DIFF_HEADS, 2 * tq, LANES), BF16),
                        pltpu.VMEM((N_DIFF_HEADS, 2 * tq, 1), F32),
                        pltpu.VMEM((N_DIFF_HEADS, 2 * tq, 1), F32),
                        pltpu.VMEM((N_DIFF_HEADS, 2 * tq, DIFF_VDIM), F32),
                        pltpu.VMEM((N_MLA_HEADS * tq, 1), F32),
                        pltpu.VMEM((N_MLA_HEADS * tq, 1), F32),
                        pltpu.VMEM((N_MLA_HEADS * tq, MLA_KV_RANK), F32)],
        compiler_params=pltpu.CompilerParams(dimension_semantics=("parallel", "arbitrary"),
                                             vmem_limit_bytes=VMEM_LIMIT),
        name="prompt_attn",
    )(lam, dq, qcat, dk, dv, kcat)


def _decode_kernel(pt_ref, lam_ref, qd_ref, qm_ref, ks_ref, vs_ref, cs_ref,
                   kt_hbm, v_hbm, c_hbm, pe_hbm, od_ref, om_ref,
                   kbuf, vbuf, cbuf, pbuf, sems, kpad, vpad, cpad, *, n_pages):
    ch = PAGES_PER_CHUNK
    n_chunks = n_pages // ch
    b = pl.program_id(0)
    nb = pl.num_programs(0)
    tok = ch * PAGE_SIZE
    n_rows = qd_ref.shape[1]
    heads_rows = n_rows // N_DIFF_HEADS

    def copies(bb, c, slot):
        out = []
        for j in range(ch):
            pg = pt_ref[bb, c * ch + j]
            out.append(pltpu.make_async_copy(kt_hbm.at[pg], kbuf.at[slot, :, pl.ds(j * PAGE_SIZE, PAGE_SIZE)],
                                             sems.at[0, slot]))
            out.append(pltpu.make_async_copy(v_hbm.at[pg], vbuf.at[slot, pl.ds(j * PAGE_SIZE * N_DIFF_HEADS,
                                                                              PAGE_SIZE * N_DIFF_HEADS), :],
                                             sems.at[1, slot]))
            out.append(pltpu.make_async_copy(c_hbm.at[pg], cbuf.at[slot, pl.ds(j * PAGE_SIZE, PAGE_SIZE), :],
                                             sems.at[2, slot]))
            out.append(pltpu.make_async_copy(pe_hbm.at[pg],
                                             pbuf.at[slot, pl.ds(0, MLA_ROPE), pl.ds(j * PAGE_SIZE, PAGE_SIZE)],
                                             sems.at[3, slot]))
        return out

    @pl.when(b == 0)
    def _():
        pbuf[...] = jnp.zeros(pbuf.shape, F32)
        kpad[...] = jnp.zeros(kpad.shape, F32)
        vpad[...] = jnp.zeros(vpad.shape, F32)
        cpad[...] = jnp.zeros(cpad.shape, F32)
        for cp in copies(0, 0, 0):
            cp.start()

    qd = qd_ref[0]
    qm = qm_ref[0]
    qm_lat = qm[:, :MLA_KV_RANK]
    qm_pe = qm[:, MLA_KV_RANK:]

    def softmax_step(s, m, l):
        m_new = jnp.maximum(m, jnp.max(s, axis=-1, keepdims=True))
        alpha = jnp.exp(m - m_new)
        p = jnp.exp(s - m_new)
        return p.astype(BF16), alpha, m_new, alpha * l + jnp.sum(p, axis=-1, keepdims=True)

    def chunk_body(c, carry):
        md, ld, accd, mm, lm, accm = carry
        g = b * n_chunks + c
        slot = g % 2
        nxt = c + 1

        @pl.when(nxt < n_chunks)
        def _():
            for cp in copies(b, nxt, 1 - slot):
                cp.start()

        @pl.when((nxt == n_chunks) & (b + 1 < nb))
        def _():
            for cp in copies(b + 1, 0, 1 - slot):
                cp.start()

        for cp in copies(b, c, slot):
            cp.wait()

        s_d = _dot(qd, kbuf[slot].astype(BF16))
        cb = cbuf[slot].astype(BF16)
        s_m = _nt_dot(qm_lat, cb) + _dot(qm_pe, pbuf[slot].astype(BF16))
        p_d, a_d, md, ld = softmax_step(s_d, md, ld)
        p_m, a_m, mm, lm = softmax_step(s_m, mm, lm)
        new_accd = []
        for h in range(N_DIFF_HEADS):
            rows = slice(heads_rows * h, heads_rows * (h + 1))
            vh = vbuf[slot, pl.ds(h, tok, stride=N_DIFF_HEADS), :].astype(BF16)
            new_accd.append(a_d[rows] * accd[h] + _dot(p_d, vh)[rows])
        accm = a_m * accm + _dot(p_m, cb)
        return md, ld, tuple(new_accd), mm, lm, accm

    init = (jnp.full((n_rows, 1), -jnp.inf, F32), jnp.zeros((n_rows, 1), F32),
            tuple(jnp.zeros((heads_rows, DIFF_VDIM), F32) for _ in range(N_DIFF_HEADS)),
            jnp.full((n_rows, 1), -jnp.inf, F32), jnp.zeros((n_rows, 1), F32),
            jnp.zeros((n_rows, MLA_KV_RANK), F32))
    md, ld, accd, mm, lm, accm = lax.fori_loop(0, n_chunks, chunk_body, init)

    n_new = ks_ref.shape[1]
    kpad[0:n_new, :] = ks_ref[0]
    vpad[0:n_new, :] = vs_ref[0]
    cpad[0:n_new, :] = cs_ref[0].astype(F32)
    qpos = lax.broadcasted_iota(jnp.int32, (n_rows, PAGE_SIZE), 0) % n_new
    kpos = lax.broadcasted_iota(jnp.int32, (n_rows, PAGE_SIZE), 1)
    ok = kpos <= qpos
    cself = cpad[...].astype(BF16)
    s_d = jnp.where(ok, _nt_dot(qd, kpad[...].astype(BF16)), -jnp.inf)
    s_m = jnp.where(ok, _nt_dot(qm, cself), -jnp.inf)
    p_d, a_d, md, ld = softmax_step(s_d, md, ld)
    p_m, a_m, mm, lm = softmax_step(s_m, mm, lm)
    pv = _dot(p_d, vpad[...].astype(BF16))
    accm = a_m * accm + _dot(p_m, cself[:, :MLA_KV_RANK])

    lam = lam_ref[...]
    half = heads_rows // 2
    for h in range(N_DIFF_HEADS):
        rows = slice(heads_rows * h, heads_rows * (h + 1))
        acc = a_d[rows] * accd[h] + pv[rows, DIFF_VDIM * h:DIFF_VDIM * (h + 1)]
        o = acc / ld[rows]
        od_ref[0, h] = o[:half] - lam * o[half:]
    om_ref[0] = accm / lm


def _decode_attn(page_table, lam, qd_blk, qm, k_self, v_self, kc_self, kt_pages, v_pages, c_pages, pe_pages):
    nb, n_pages = page_table.shape
    n_rows = qd_blk.shape[1]
    n_new = k_self.shape[1]
    ch = PAGES_PER_CHUNK
    tok = ch * PAGE_SIZE
    row_blk = lambda w: pl.BlockSpec((1, n_rows, w), lambda b, pt: (b, 0, 0))
    new_blk = lambda w: pl.BlockSpec((1, n_new, w), lambda b, pt: (b, 0, 0))
    any_spec = pl.BlockSpec(memory_space=pl.ANY)
    grid_spec = pltpu.PrefetchScalarGridSpec(
        num_scalar_prefetch=1,
        grid=(nb,),
        in_specs=[pl.BlockSpec((1, LANES), lambda b, pt: (0, 0)),
                  row_blk(DQ_COLS), row_blk(QCAT), new_blk(DQ_COLS), new_blk(DQ_COLS), new_blk(QCAT),
                  any_spec, any_spec, any_spec, any_spec],
        out_specs=[pl.BlockSpec((1, N_DIFF_HEADS, n_new, DIFF_VDIM), lambda b, pt: (b, 0, 0, 0)),
                   pl.BlockSpec((1, n_rows, MLA_KV_RANK), lambda b, pt: (b, 0, 0))],
        scratch_shapes=[pltpu.VMEM((2, DQ_COLS, tok), F32),
                        pltpu.VMEM((2, tok * N_DIFF_HEADS, DIFF_VDIM), F32),
                        pltpu.VMEM((2, tok, MLA_KV_RANK), F32),
                        pltpu.VMEM((2, LANES, tok), F32),
                        pltpu.SemaphoreType.DMA((4, 2)),
                        pltpu.VMEM((PAGE_SIZE, DQ_COLS), F32),
                        pltpu.VMEM((PAGE_SIZE, DQ_COLS), F32),
                        pltpu.VMEM((PAGE_SIZE, QCAT), F32)])
    return pl.pallas_call(
        functools.partial(_decode_kernel, n_pages=n_pages),
        grid_spec=grid_spec,
        out_shape=[jax.ShapeDtypeStruct((nb, N_DIFF_HEADS, n_new, DIFF_VDIM), F32),
                   jax.ShapeDtypeStruct((nb, n_rows, MLA_KV_RANK), F32)],
        compiler_params=pltpu.CompilerParams(dimension_semantics=("arbitrary",),
                                             vmem_limit_bytes=VMEM_LIMIT),
        name="decode_attn",
    )(page_table, lam, qd_blk, qm, k_self, v_self, kc_self, kt_pages, v_pages, c_pages, pe_pages)


def _out_kernel(x_ref, g_ref, od_ref, om_ref, sub_ref, wuv_ref, wo_ref, o_ref):
    parts = []
    for h in range(N_DIFF_HEADS):
        o = od_ref[0, :, DIFF_VDIM * h:DIFF_VDIM * (h + 1)]
        parts.append((_rms(o, sub_ref[...]) * (1.0 - LAMBDA_INIT)).astype(BF16))
    pair = 2 * MLA_KV_RANK
    for j in range(N_MLA_HEADS // 2):
        parts.append(_dot(om_ref[0, :, pair * j:pair * (j + 1)], wuv_ref[j]).astype(BF16))
    mixed = jnp.concatenate(parts, axis=1)
    o_ref[0] = x_ref[0] + g_ref[0] * _dot(mixed, wo_ref[...])


def _out_proj(x, mod, od, om, diff_subln, w_uv_pairs, w_o, *, tm):
    nb, t, _ = x.shape
    r = mod.shape[1]
    rb = 1 if r == 1 else tm
    gate_idx = (lambda b, i: (b, 0, 2)) if r == 1 else (lambda b, i: (b, i, 2))
    tok = lambda w: pl.BlockSpec((1, tm, w), lambda b, i: (b, i, 0))
    return pl.pallas_call(
        _out_kernel,
        grid=(nb, t // tm),
        in_specs=[tok(D_MODEL), pl.BlockSpec((1, rb, D_MODEL), gate_idx),
                  tok(DIFF_WIDTH), tok(N_MLA_HEADS * MLA_KV_RANK),
                  _const_spec((1, DIFF_VDIM)),
                  _const_spec((N_MLA_HEADS // 2, 2 * MLA_KV_RANK, LANES)),
                  _const_spec((DIFF_WIDTH + MLA_WIDTH, D_MODEL))],
        out_specs=tok(D_MODEL),
        out_shape=jax.ShapeDtypeStruct((nb, t, D_MODEL), F32),
        compiler_params=pltpu.CompilerParams(dimension_semantics=("parallel", "parallel"),
                                             vmem_limit_bytes=VMEM_LIMIT),
        name="out_proj",
    )(x, mod, od, om, diff_subln, w_uv_pairs, w_o)


FF_CHUNK = 256


def _ffn_kernel(x_ref, sh_ref, sc_ref, g_ref, fn_ref, wgu_ref, wd_ref, final_ref, y_ref):
    x = x_ref[0]
    h = (_rms(x, fn_ref[...]) * (1.0 + sc_ref[0]) + sh_ref[0]).astype(BF16)
    acc = jnp.zeros(x.shape, F32)
    for c in range(D_FF // FF_CHUNK):
        lo = FF_CHUNK * c
        gate = _dot(h, wgu_ref[:, lo:lo + FF_CHUNK])
        up = _dot(h, wgu_ref[:, D_FF + lo:D_FF + lo + FF_CHUNK])
        act = (gate * jax.nn.sigmoid(gate) * up).astype(BF16)
        acc = acc + _dot(act, wd_ref[lo:lo + FF_CHUNK, :])
    y_ref[0] = _rms(x + g_ref[0] * acc, final_ref[...])


def _ffn(x, mod, ffn_norm, w_gu, w_down, final_norm, *, tm):
    nb, t, _ = x.shape
    r = mod.shape[1]
    rb = 1 if r == 1 else tm
    mod_idx = (lambda col: (lambda b, i: (b, 0, col))) if r == 1 else (lambda col: (lambda b, i: (b, i, col)))
    tok = pl.BlockSpec((1, tm, D_MODEL), lambda b, i: (b, i, 0))
    return pl.pallas_call(
        _ffn_kernel,
        grid=(nb, t // tm),
        in_specs=[tok,
                  pl.BlockSpec((1, rb, D_MODEL), mod_idx(3)),
                  pl.BlockSpec((1, rb, D_MODEL), mod_idx(4)),
                  pl.BlockSpec((1, rb, D_MODEL), mod_idx(5)),
                  _const_spec((1, D_MODEL)),
                  _const_spec((D_MODEL, 2 * D_FF)),
                  _const_spec((D_FF, D_MODEL)),
                  _const_spec((1, D_MODEL))],
        out_specs=tok,
        out_shape=jax.ShapeDtypeStruct((nb, t, D_MODEL), F32),
        compiler_params=pltpu.CompilerParams(dimension_semantics=("parallel", "parallel"),
                                             vmem_limit_bytes=VMEM_LIMIT),
        name="ffn",
    )(x, mod, mod, mod, ffn_norm, w_gu, w_down, final_norm)


def _rope_tables(pos, rot, period):
    half = rot // 2
    inv = ROPE_THETA ** (-(jnp.arange(half, dtype=F32) * 2.0 / rot))
    ang = pos.astype(F32)[:, None] * inv[None, :]
    cos, sin = jnp.cos(ang), jnp.sin(ang)
    j = jnp.arange(LANES) % period
    first = j < half
    second = (j >= half) & (j < rot)
    idx = jnp.where(first, j, jnp.clip(j - half, 0, half - 1))
    c = jnp.where(first | second, cos[:, idx], 1.0)
    s1 = jnp.where(first, -sin[:, idx], 0.0)
    s2 = jnp.where(second, sin[:, idx], 0.0)
    return jnp.stack([c, s1, s2])


def _prep_weights(w_in, w_uq, w_ukv, w_o, w_gate_up, w_down):
    w_in_p = jnp.pad(w_in, ((0, 0), (0, IN_COLS_PAD - IN_COLS))).astype(BF16)
    uq = w_uq.reshape(MLA_Q_RANK, N_MLA_HEADS, MLA_NOPE + MLA_ROPE)
    w_uq_p = jnp.concatenate([uq[:, :, :MLA_NOPE].reshape(MLA_Q_RANK, -1),
                              uq[:, :, MLA_NOPE:].reshape(MLA_Q_RANK, -1)], axis=1).astype(BF16)
    ukv = w_ukv.reshape(MLA_KV_RANK, N_MLA_HEADS, MLA_NOPE + MLA_V)
    uk_t = jnp.transpose(ukv[:, :, :MLA_NOPE], (1, 2, 0))
    uv = jnp.transpose(ukv[:, :, MLA_NOPE:], (1, 0, 2))
    zk = jnp.zeros_like(uk_t[0])
    zv = jnp.zeros_like(uv[0])
    uk_pairs = jnp.stack([jnp.block([[uk_t[2 * j], zk], [zk, uk_t[2 * j + 1]]])
                          for j in range(N_MLA_HEADS // 2)]).astype(BF16)
    uv_pairs = jnp.stack([jnp.block([[uv[2 * j], zv], [zv, uv[2 * j + 1]]])
                          for j in range(N_MLA_HEADS // 2)]).astype(BF16)
    return w_in_p, w_uq_p, uk_pairs, uv_pairs, w_o.astype(BF16), w_gate_up.astype(BF16), w_down.astype(BF16)


def kernel(x_prompt, x_sample, c_prompt, c_sample, cache_k_diff, cache_v_diff, cache_ckv, cache_kpe, page_table, mod_w, mod_b, attn_norm, w_in, q_norm, kv_norm, w_uq, w_ukv, lambda_q1, lambda_k1, lambda_q2, lambda_k2, diff_subln, w_o, ffn_norm, w_gate_up, w_down, final_norm):
    assert mod_w.shape[0] == 1, "single-layer trunk"
    bp, tp, _ = x_prompt.shape
    bs, ts, _ = x_sample.shape
    n_pool = cache_ckv.shape[1]
    n_pages = page_table.shape[1]
    assert n_pages % PAGES_PER_CHUNK == 0
    ns = bs * ts

    w_in_p, w_uq_p, uk_pairs, uv_pairs, w_o_b, w_gu_b, w_down_b = _prep_weights(
        w_in[0], w_uq[0], w_ukv[0], w_o[0], w_gate_up[0], w_down[0])
    final_norm2 = final_norm.reshape(1, D_MODEL)

    lam_vecs = jnp.concatenate([lambda_q1, lambda_k1, lambda_q2, lambda_k2], axis=0)
    mod, lam = _modulation(jnp.concatenate([c_prompt, c_sample], axis=0), mod_w[0], mod_b, lam_vecs)
    mod_p = mod[:bp].reshape(bp, 1, 6 * D_MODEL)
    mod_s = jnp.repeat(mod[bp:], ts, axis=0).reshape(1, ns, 6 * D_MODEL)

    pos_p = jnp.arange(tp)
    pos_s = n_pages * PAGE_SIZE + (jnp.arange(ns) % ts)
    shared = (attn_norm, w_in_p, q_norm, kv_norm, w_uq_p, uk_pairs)

    tm_p = min(256, tp)
    dq, dk_f, dk_b, dv_f, dv_b, ckv_f, kpe_f, kcat, qcat = _in_proj(
        x_prompt, mod_p, _rope_tables(pos_p, DIFF_ROT, DIFF_DH), _rope_tables(pos_p, MLA_ROPE, MLA_ROPE),
        *shared, tm=tm_p)
    od_p, om_p = _prompt_attn(lam, dq, qcat, dk_b, dv_b, kcat, tq=tm_p)
    tm_f = min(512, tp)
    x1_p = _out_proj(x_prompt, mod_p, od_p, om_p, diff_subln, uv_pairs, w_o_b, tm=tm_f)
    y_prompt = _ffn(x1_p, mod_p, ffn_norm, w_gu_b, w_down_b, final_norm2, tm=tm_f)

    xs = x_sample.reshape(1, ns, D_MODEL)
    tm_s = min(256, ns)
    sq, sk_f, _, sv_f, _, sckv_f, skpe_f, skcat, sqcat = _in_proj(
        xs, mod_s, _rope_tables(pos_s, DIFF_ROT, DIFF_DH), _rope_tables(pos_s, MLA_ROPE, MLA_ROPE),
        *shared, tm=tm_s)
    n_hm = 2 * N_DIFF_HEADS
    sq5 = jnp.transpose(sq.reshape(bs, ts, n_hm, DIFF_DH), (0, 2, 1, 3))
    qd_blk = (sq5[:, :, :, None, :] * jnp.eye(n_hm, dtype=BF16)[None, :, None, :, None]
              ).reshape(bs, n_hm * ts, DQ_COLS)
    qm = jnp.transpose(sqcat.reshape(N_MLA_HEADS, bs, ts, QCAT), (1, 0, 2, 3)).reshape(bs, N_MLA_HEADS * ts, QCAT)
    kt_pages = jnp.transpose(cache_k_diff[0], (0, 2, 3, 4, 1)).reshape(n_pool, DQ_COLS, PAGE_SIZE)
    v_pages = cache_v_diff[0].reshape(n_pool, PAGE_SIZE * N_DIFF_HEADS, DIFF_VDIM)
    pe_pages = jnp.transpose(cache_kpe[0], (0, 2, 1))
    od_s, om_s = _decode_attn(page_table, lam, qd_blk, qm,
                              sk_f.reshape(bs, ts, DQ_COLS), sv_f.reshape(bs, ts, DQ_COLS),
                              skcat.reshape(bs, ts, QCAT), kt_pages, v_pages, cache_ckv[0], pe_pages)
    od_s = jnp.transpose(od_s, (0, 2, 1, 3)).reshape(1, ns, DIFF_WIDTH)
    om_s = jnp.transpose(om_s.reshape(bs, N_MLA_HEADS, ts, MLA_KV_RANK), (0, 2, 1, 3)
                         ).reshape(1, ns, N_MLA_HEADS * MLA_KV_RANK).astype(BF16)
    tm_fs = min(512, ns)
    x1_s = _out_proj(xs, mod_s, od_s, om_s, diff_subln, uv_pairs, w_o_b, tm=tm_fs)
    y_sample = _ffn(x1_s, mod_s, ffn_norm, w_gu_b, w_down_b, final_norm2, tm=tm_fs).reshape(bs, ts, D_MODEL)

    return (y_prompt, y_sample,
            dk_f.reshape(1, bp, tp, N_DIFF_HEADS, 2, DIFF_DH),
            dv_f.reshape(1, bp, tp, N_DIFF_HEADS, DIFF_VDIM),
            ckv_f.reshape(1, bp, tp, MLA_KV_RANK),
            kpe_f.reshape(1, bp, tp, MLA_ROPE),
            sk_f.reshape(1, bs, ts, N_DIFF_HEADS, 2, DIFF_DH),
            sv_f.reshape(1, bs, ts, N_DIFF_HEADS, DIFF_VDIM),
            sckv_f.reshape(1, bs, ts, MLA_KV_RANK),
            skpe_f.reshape(1, bs, ts, MLA_ROPE))
```

```python
import functools

import jax
import jax.numpy as jnp
from jax import lax
from jax.experimental import pallas as pl
from jax.experimental.pallas import tpu as pltpu

F32 = jnp.float32
BF16 = jnp.bfloat16

D_MODEL = 1024
N_DIFF_HEADS = 4
DIFF_DH = 64
DIFF_VDIM = 2 * DIFF_DH
DIFF_ROT = DIFF_DH // 4
N_MLA_HEADS = 8
MLA_NOPE = 64
MLA_ROPE = 32
MLA_V = 64
MLA_Q_RANK = 384
MLA_KV_RANK = 256
DIFF_WIDTH = N_DIFF_HEADS * DIFF_VDIM
MLA_WIDTH = N_MLA_HEADS * MLA_V
DQ_COLS = N_DIFF_HEADS * 2 * DIFF_DH
IN_COLS = 3 * DQ_COLS + MLA_Q_RANK + MLA_KV_RANK + MLA_ROPE
D_FF = 2816
ROPE_THETA = 500000.0
NORM_EPS = 1e-6
PAGE_SIZE = 128
LOG2E = 1.4426950408889634
DIFF_SCALE = DIFF_DH ** -0.5 * LOG2E
MLA_SCALE = (MLA_NOPE + MLA_ROPE) ** -0.5 * LOG2E
LAMBDA_INIT = 0.8 - 0.6

LANES = 128
IN_COLS_PAD = 2304
QCAT = MLA_KV_RANK + LANES
PAGES_PER_CHUNK = 8
VMEM_LIMIT = 56 * 1024 * 1024


def _nt_dot(a, b):
    return lax.dot_general(a, b, (((1,), (1,)), ((), ())), preferred_element_type=F32)


def _dot(a, b):
    return jnp.dot(a, b, preferred_element_type=F32)


def _rms(x, g):
    return x * lax.rsqrt(jnp.mean(x * x, axis=-1, keepdims=True) + NORM_EPS) * g


def _rope(x, tab_ref, half):
    return (x * tab_ref[0]
            + pltpu.roll(x, LANES - half, 1) * tab_ref[1]
            + pltpu.roll(x, half, 1) * tab_ref[2])


def _const_spec(shape):
    nd = len(shape)
    return pl.BlockSpec(shape, lambda *_: (0,) * nd, pipeline_mode=pl.Buffered(1))


def _mod_kernel(c_ref, w_ref, b_ref, lv_ref, m_ref, lam_ref):
    c = c_ref[...]
    a = (c * jax.nn.sigmoid(c)).astype(BF16)
    m_ref[...] = _dot(a, w_ref[...].astype(BF16)) + b_ref[...]
    lv = lv_ref[...]
    s1 = jnp.sum(lv[0:1] * lv[1:2], axis=-1, keepdims=True)
    s2 = jnp.sum(lv[2:3] * lv[3:4], axis=-1, keepdims=True)
    lam_ref[...] = jnp.broadcast_to(jnp.exp(s1) - jnp.exp(s2) + LAMBDA_INIT, lam_ref.shape)


def _modulation(c_all, mod_w, mod_b, lam_vecs):
    n = c_all.shape[0]
    tn = 1024
    return pl.pallas_call(
        _mod_kernel,
        grid=(6 * D_MODEL // tn,),
        in_specs=[pl.BlockSpec((n, D_MODEL), lambda j: (0, 0)),
                  pl.BlockSpec((D_MODEL, tn), lambda j: (0, j)),
                  pl.BlockSpec((1, tn), lambda j: (0, j)),
                  pl.BlockSpec((4, DIFF_DH), lambda j: (0, 0))],
        out_specs=[pl.BlockSpec((n, tn), lambda j: (0, j)),
                   pl.BlockSpec((1, LANES), lambda j: (0, 0))],
        out_shape=[jax.ShapeDtypeStruct((n, 6 * D_MODEL), F32),
                   jax.ShapeDtypeStruct((1, LANES), F32)],
        compiler_params=pltpu.CompilerParams(dimension_semantics=("arbitrary",)),
        name="modulation",
    )(c_all, mod_w, mod_b, lam_vecs)


def _in_kernel(x_ref, sh_ref, sc_ref, rd_ref, rp_ref, an_ref, win_ref, qn_ref, kn_ref, wuq_ref, wuk_ref,
               dq_ref, dkf_ref, dkb_ref, dvf_ref, dvb_ref, ckvf_ref, kpef_ref, kcat_ref, qcat_ref):
    x = x_ref[0]
    tm = x.shape[0]
    h = (_rms(x, an_ref[...]) * (1.0 + sc_ref[0]) + sh_ref[0]).astype(BF16)
    z = _dot(h, win_ref[...])
    for j in range(DQ_COLS // LANES):
        sl = slice(LANES * j, LANES * (j + 1))
        dq_ref[0, :, sl] = (_rope(z[:, sl], rd_ref, DIFF_ROT // 2) * DIFF_SCALE).astype(BF16)
        k = _rope(z[:, DQ_COLS + LANES * j:DQ_COLS + LANES * (j + 1)], rd_ref, DIFF_ROT // 2)
        dkf_ref[0, :, sl] = k
        dkb_ref[0, :, sl] = k.astype(BF16)
    dv = z[:, 2 * DQ_COLS:3 * DQ_COLS]
    for hh in range(N_DIFF_HEADS):
        dvf_ref[0, pl.ds(hh, tm, stride=N_DIFF_HEADS), :] = dv[:, DIFF_VDIM * hh:DIFF_VDIM * (hh + 1)]
    dvb_ref[0] = dv.astype(BF16)

    c0 = 3 * DQ_COLS
    cqn = _rms(z[:, c0:c0 + MLA_Q_RANK], qn_ref[...]).astype(BF16)
    q = _dot(cqn, wuq_ref[...])
    nope_w = N_MLA_HEADS * MLA_NOPE
    qn = q[:, :nope_w].astype(BF16)
    for j in range(N_MLA_HEADS // 2):
        ql = _dot(qn[:, LANES * j:LANES * (j + 1)], wuk_ref[j])
        qcat_ref[0, 2 * j, :, 0:MLA_KV_RANK] = (ql[:, :MLA_KV_RANK] * MLA_SCALE).astype(BF16)
        qcat_ref[0, 2 * j + 1, :, 0:MLA_KV_RANK] = (ql[:, MLA_KV_RANK:] * MLA_SCALE).astype(BF16)
    lane = lax.broadcasted_iota(jnp.int32, (tm, LANES), 1)
    heads_per_slab = LANES // MLA_ROPE
    for g in range(N_MLA_HEADS // heads_per_slab):
        qp = _rope(q[:, nope_w + LANES * g:nope_w + LANES * (g + 1)], rp_ref, MLA_ROPE // 2) * MLA_SCALE
        for i in range(heads_per_slab):
            moved = qp if i == 0 else pltpu.roll(qp, LANES - MLA_ROPE * i, 1)
            qcat_ref[0, heads_per_slab * g + i, :, MLA_KV_RANK:QCAT] = (
                jnp.where(lane < MLA_ROPE, moved, 0.0).astype(BF16))

    c1 = c0 + MLA_Q_RANK
    ckvn = _rms(z[:, c1:c1 + MLA_KV_RANK], kn_ref[...])
    ckvf_ref[0] = ckvn
    kpe = _rope(z[:, c1 + MLA_KV_RANK:IN_COLS_PAD], rp_ref, MLA_ROPE // 2)
    kpef_ref[0] = kpe[:, :MLA_ROPE]
    kcat_ref[0, :, 0:MLA_KV_RANK] = ckvn.astype(BF16)
    kcat_ref[0, :, MLA_KV_RANK:QCAT] = kpe.astype(BF16)


def _in_proj(x, mod, rope_d, rope_p, attn_norm, w_in_p, q_norm, kv_norm, w_uq_p, w_uk_pairs, *, tm):
    nb, t, _ = x.shape
    r = mod.shape[1]
    rb = 1 if r == 1 else tm
    mod_idx = (lambda col: (lambda b, i: (b, 0, col))) if r == 1 else (lambda col: (lambda b, i: (b, i, col)))
    tok = lambda w: pl.BlockSpec((1, tm, w), lambda b, i: (b, i, 0))
    rope_spec = pl.BlockSpec((3, tm, LANES), lambda b, i: (0, i, 0))
    out_shape = [jax.ShapeDtypeStruct((nb, t, DQ_COLS), BF16),
                 jax.ShapeDtypeStruct((nb, t, DQ_COLS), F32), jax.ShapeDtypeStruct((nb, t, DQ_COLS), BF16),
                 jax.ShapeDtypeStruct((nb, t * N_DIFF_HEADS, DIFF_VDIM), F32),
                 jax.ShapeDtypeStruct((nb, t, DQ_COLS), BF16),
                 jax.ShapeDtypeStruct((nb, t, MLA_KV_RANK), F32),
                 jax.ShapeDtypeStruct((nb, t, MLA_ROPE), F32),
                 jax.ShapeDtypeStruct((nb, t, QCAT), BF16),
                 jax.ShapeDtypeStruct((nb, N_MLA_HEADS, t, QCAT), BF16)]
    out_specs = [tok(DQ_COLS), tok(DQ_COLS), tok(DQ_COLS),
                 pl.BlockSpec((1, tm * N_DIFF_HEADS, DIFF_VDIM), lambda b, i: (b, i, 0)), tok(DQ_COLS),
                 tok(MLA_KV_RANK), tok(MLA_ROPE), tok(QCAT),
                 pl.BlockSpec((1, N_MLA_HEADS, tm, QCAT), lambda b, i: (b, 0, i, 0))]
    return pl.pallas_call(
        _in_kernel,
        grid=(nb, t // tm),
        in_specs=[tok(D_MODEL),
                  pl.BlockSpec((1, rb, D_MODEL), mod_idx(0)),
                  pl.BlockSpec((1, rb, D_MODEL), mod_idx(1)),
                  rope_spec, rope_spec,
                  _const_spec((1, D_MODEL)),
                  _const_spec((D_MODEL, IN_COLS_PAD)),
                  _const_spec((1, MLA_Q_RANK)),
                  _const_spec((1, MLA_KV_RANK)),
                  _const_spec((MLA_Q_RANK, N_MLA_HEADS * (MLA_NOPE + MLA_ROPE))),
                  _const_spec((N_MLA_HEADS // 2, LANES, 2 * MLA_KV_RANK))],
        out_specs=out_specs,
        out_shape=out_shape,
        compiler_params=pltpu.CompilerParams(dimension_semantics=("parallel", "parallel"),
                                             vmem_limit_bytes=VMEM_LIMIT),
        name="in_proj",
    )(x, mod, mod, rope_d, rope_p, attn_norm, w_in_p, q_norm, kv_norm, w_uq_p, w_uk_pairs)


def _online_update(s, m_ref, l_ref, acc_ref, v, idx):
    blocks = [s[:, LANES * j:LANES * (j + 1)] for j in range(s.shape[1] // LANES)]
    m_old = m_ref[idx]
    m_new = jnp.maximum(m_old, jnp.max(functools.reduce(jnp.maximum, blocks), axis=-1, keepdims=True))
    alpha = jnp.exp2(m_old - m_new)
    p = [jnp.exp2(blk - m_new) for blk in blocks]
    l_ref[idx] = alpha * l_ref[idx] + functools.reduce(jnp.add, p)
    pv = _dot(jnp.concatenate(p, axis=1).astype(BF16), v)
    acc = acc_ref[idx]
    scale = jnp.concatenate([alpha] * (acc.shape[1] // LANES), axis=1)
    acc_ref[idx] = scale * acc + pv
    m_ref[idx] = m_new


def _prompt_attn_kernel(lam_ref, dq_ref, qcat_ref, dk_ref, dv_ref, kcat_ref, od_ref, om_ref,
                        qd_s, md_s, ld_s, accd_s, mm_s, lm_s, accm_s, *, tq):
    qi = pl.program_id(1)
    lane = lax.broadcasted_iota(jnp.int32, (tq, LANES), 1)
    for h in range(N_DIFF_HEADS):
        qh = dq_ref[0, :, LANES * h:LANES * (h + 1)]
        qd_s[h, 0:tq, :] = jnp.where(lane < DIFF_DH, qh, jnp.zeros_like(qh))
        qd_s[h, tq:2 * tq, :] = jnp.where(lane >= DIFF_DH, qh, jnp.zeros_like(qh))
    md_s[...] = jnp.full(md_s.shape, -jnp.inf, F32)
    ld_s[...] = jnp.zeros(ld_s.shape, F32)
    accd_s[...] = jnp.zeros(accd_s.shape, F32)
    mm_s[...] = jnp.full(mm_s.shape, -jnp.inf, F32)
    lm_s[...] = jnp.zeros(lm_s.shape, F32)
    accm_s[...] = jnp.zeros(accm_s.shape, F32)

    def key_tile(kt, diagonal):
        k0 = pl.multiple_of(kt * tq, tq)

        def causal(s):
            if not diagonal:
                return s
            r = lax.broadcasted_iota(jnp.int32, s.shape, 0) % tq
            c = lax.broadcasted_iota(jnp.int32, s.shape, 1)
            return jnp.where(c <= r, s, -jnp.inf)

        for h in range(N_DIFF_HEADS):
            sl = slice(LANES * h, LANES * (h + 1))
            s = causal(_nt_dot(qd_s[h], dk_ref[0, pl.ds(k0, tq), sl]))
            _online_update(s, md_s, ld_s, accd_s, dv_ref[0, pl.ds(k0, tq), sl], h)
        kc = kcat_ref[0, pl.ds(k0, tq), :]
        s = causal(_nt_dot(qcat_ref[0].reshape(N_MLA_HEADS * tq, QCAT), kc))
        _online_update(s, mm_s, lm_s, accm_s, kc[:, :MLA_KV_RANK], slice(None))

    def body(kt, carry):
        key_tile(kt, False)
        return carry

    lax.fori_loop(0, qi, body, 0)
    key_tile(qi, True)

    lam = lam_ref[...]
    for h in range(N_DIFF_HEADS):
        o = accd_s[h] / jnp.sum(ld_s[h], axis=-1, keepdims=True)
        od_ref[0, :, LANES * h:LANES * (h + 1)] = o[:tq] - lam * o[tq:]
    om = accm_s[...] / jnp.sum(lm_s[...], axis=-1, keepdims=True)
    for h in range(N_MLA_HEADS):
        om_ref[0, :, MLA_KV_RANK * h:MLA_KV_RANK * (h + 1)] = om[h * tq:(h + 1) * tq].astype(BF16)


def _prompt_attn(lam, dq, qcat, dk, dv, kcat, *, tq):
    b, t, _ = dq.shape
    kv = lambda w: pl.BlockSpec((1, t, w), lambda bi, qi: (bi, 0, 0))
    return pl.pallas_call(
        functools.partial(_prompt_attn_kernel, tq=tq),
        grid=(b, t // tq),
        in_specs=[pl.BlockSpec((1, LANES), lambda bi, qi: (0, 0)),
                  pl.BlockSpec((1, tq, DQ_COLS), lambda bi, qi: (bi, qi, 0)),
                  pl.BlockSpec((1, N_MLA_HEADS, tq, QCAT), lambda bi, qi: (bi, 0, qi, 0)),
                  kv(DQ_COLS), kv(DQ_COLS), kv(QCAT)],
        out_specs=[pl.BlockSpec((1, tq, DIFF_WIDTH), lambda bi, qi: (bi, qi, 0)),
                   pl.BlockSpec((1, tq, N_MLA_HEADS * MLA_KV_RANK), lambda bi, qi: (bi, qi, 0))],
        out_shape=[jax.ShapeDtypeStruct((b, t, DIFF_WIDTH), F32),
                   jax.ShapeDtypeStruct((b, t, N_MLA_HEADS * MLA_KV_RANK), BF16)],
        scratch_shapes=[pltpu.VMEM((N_DIFF_HEADS, 2 * tq, LANES), BF16),
                        pltpu.VMEM((N_DIFF_HEADS, 2 * tq, LANES), F32),
                        pltpu.VMEM((N_DIFF_HEADS, 2 * tq, LANES), F32),
                        pltpu.VMEM((N_DIFF_HEADS, 2 * tq, DIFF_VDIM), F32),
                        pltpu.VMEM((N_MLA_HEADS * tq, LANES), F32),
                        pltpu.VMEM((N_MLA_HEADS * tq, LANES), F32),
                        pltpu.VMEM((N_MLA_HEADS * tq, MLA_KV_RANK), F32)],
        compiler_params=pltpu.CompilerParams(dimension_semantics=("parallel", "arbitrary"),
                                             vmem_limit_bytes=VMEM_LIMIT),
        name="prompt_attn",
    )(lam, dq, qcat, dk, dv, kcat)


def _decode_kernel(pt_ref, lam_ref, qd_ref, qm_ref, ks_ref, vs_ref, cs_ref,
                   kt_hbm, v_hbm, c_hbm, pe_hbm, od_ref, om_ref,
                   kbuf, vbuf, cbuf, pbuf, sems, kpad, vpad, cpad, *, n_pages):
    ch = PAGES_PER_CHUNK
    n_chunks = n_pages // ch
    b = pl.program_id(0)
    nb = pl.num_programs(0)
    tok = ch * PAGE_SIZE
    n_rows = qd_ref.shape[1]
    heads_rows = n_rows // N_DIFF_HEADS

    def copies(bb, c, slot):
        out = []
        for j in range(ch):
            pg = pt_ref[bb, c * ch + j]
            out.append(pltpu.make_async_copy(kt_hbm.at[pg], kbuf.at[slot, :, pl.ds(j * PAGE_SIZE, PAGE_SIZE)],
                                             sems.at[0, slot]))
            out.append(pltpu.make_async_copy(v_hbm.at[pg], vbuf.at[slot, pl.ds(j * PAGE_SIZE * N_DIFF_HEADS,
                                                                              PAGE_SIZE * N_DIFF_HEADS), :],
                                             sems.at[1, slot]))
            out.append(pltpu.make_async_copy(c_hbm.at[pg], cbuf.at[slot, pl.ds(j * PAGE_SIZE, PAGE_SIZE), :],
                                             sems.at[2, slot]))
            out.append(pltpu.make_async_copy(pe_hbm.at[pg],
                                             pbuf.at[slot, pl.ds(0, MLA_ROPE), pl.ds(j * PAGE_SIZE, PAGE_SIZE)],
                                             sems.at[3, slot]))
        return out

    def start(cps):
        for i, cp in enumerate(cps):
            cp.start(priority=i % 2)

    @pl.when(b == 0)
    def _():
        pbuf[...] = jnp.zeros(pbuf.shape, F32)
        kpad[...] = jnp.zeros(kpad.shape, F32)
        vpad[...] = jnp.zeros(vpad.shape, F32)
        cpad[...] = jnp.zeros(cpad.shape, F32)
        start(copies(0, 0, 0))

    qd = qd_ref[0]
    qm = qm_ref[0]
    qm_lat = qm[:, :MLA_KV_RANK]
    qm_pe = qm[:, MLA_KV_RANK:]

    def softmax_step(s, m, l):
        m_new = jnp.maximum(m, jnp.max(s, axis=-1, keepdims=True))
        alpha = jnp.exp2(m - m_new)
        p = jnp.exp2(s - m_new)
        return p.astype(BF16), alpha, m_new, alpha * l + jnp.sum(p, axis=-1, keepdims=True)

    def chunk_body(c, carry):
        md, ld, accd, mm, lm, accm = carry
        g = b * n_chunks + c
        slot = g % 2
        nxt = c + 1

        @pl.when(g + 1 < nb * n_chunks)
        def _():
            wrap = nxt == n_chunks
            start(copies(jnp.where(wrap, b + 1, b), jnp.where(wrap, 0, nxt), 1 - slot))

        for cp in copies(b, c, slot):
            cp.wait()

        s_d = _dot(qd, kbuf[slot].astype(BF16))
        cb = cbuf[slot].astype(BF16)
        s_m = _nt_dot(qm_lat, cb) + _dot(qm_pe, pbuf[slot].astype(BF16))
        p_d, a_d, md, ld = softmax_step(s_d, md, ld)
        p_m, a_m, mm, lm = softmax_step(s_m, mm, lm)
        new_accd = []
        for h in range(N_DIFF_HEADS):
            rows = slice(heads_rows * h, heads_rows * (h + 1))
            vh = vbuf[slot, pl.ds(h, tok, stride=N_DIFF_HEADS), :].astype(BF16)
            new_accd.append(a_d[rows] * accd[h] + _dot(p_d, vh)[rows])
        accm = a_m * accm + _dot(p_m, cb)
        return md, ld, tuple(new_accd), mm, lm, accm

    init = (jnp.full((n_rows, 1), -jnp.inf, F32), jnp.zeros((n_rows, 1), F32),
            tuple(jnp.zeros((heads_rows, DIFF_VDIM), F32) for _ in range(N_DIFF_HEADS)),
            jnp.full((n_rows, 1), -jnp.inf, F32), jnp.zeros((n_rows, 1), F32),
            jnp.zeros((n_rows, MLA_KV_RANK), F32))
    md, ld, accd, mm, lm, accm = lax.fori_loop(0, n_chunks, chunk_body, init)

    n_new = ks_ref.shape[1]
    kpad[0:n_new, :] = ks_ref[0]
    vpad[0:n_new, :] = vs_ref[0]
    cpad[0:n_new, :] = cs_ref[0].astype(F32)
    qpos = lax.broadcasted_iota(jnp.int32, (n_rows, PAGE_SIZE), 0) % n_new
    kpos = lax.broadcasted_iota(jnp.int32, (n_rows, PAGE_SIZE), 1)
    ok = kpos <= qpos
    cself = cpad[...].astype(BF16)
    s_d = jnp.where(ok, _nt_dot(qd, kpad[...].astype(BF16)), -jnp.inf)
    s_m = jnp.where(ok, _nt_dot(qm, cself), -jnp.inf)
    p_d, a_d, md, ld = softmax_step(s_d, md, ld)
    p_m, a_m, mm, lm = softmax_step(s_m, mm, lm)
    pv = _dot(p_d, vpad[...].astype(BF16))
    accm = a_m * accm + _dot(p_m, cself[:, :MLA_KV_RANK])

    lam = lam_ref[...]
    half = heads_rows // 2
    for h in range(N_DIFF_HEADS):
        rows = slice(heads_rows * h, heads_rows * (h + 1))
        acc = a_d[rows] * accd[h] + pv[rows, DIFF_VDIM * h:DIFF_VDIM * (h + 1)]
        o = acc / ld[rows]
        od_ref[0, h] = o[:half] - lam * o[half:]
    om_ref[0] = accm / lm


def _decode_attn(page_table, lam, qd_blk, qm, k_self, v_self, kc_self, kt_pages, v_pages, c_pages, pe_pages):
    nb, n_pages = page_table.shape
    n_rows = qd_blk.shape[1]
    n_new = k_self.shape[1]
    ch = PAGES_PER_CHUNK
    tok = ch * PAGE_SIZE
    row_blk = lambda w: pl.BlockSpec((1, n_rows, w), lambda b, pt: (b, 0, 0))
    new_blk = lambda w: pl.BlockSpec((1, n_new, w), lambda b, pt: (b, 0, 0))
    any_spec = pl.BlockSpec(memory_space=pl.ANY)
    grid_spec = pltpu.PrefetchScalarGridSpec(
        num_scalar_prefetch=1,
        grid=(nb,),
        in_specs=[pl.BlockSpec((1, LANES), lambda b, pt: (0, 0)),
                  row_blk(DQ_COLS), row_blk(QCAT), new_blk(DQ_COLS), new_blk(DQ_COLS), new_blk(QCAT),
                  any_spec, any_spec, any_spec, any_spec],
        out_specs=[pl.BlockSpec((1, N_DIFF_HEADS, n_new, DIFF_VDIM), lambda b, pt: (b, 0, 0, 0)),
                   pl.BlockSpec((1, n_rows, MLA_KV_RANK), lambda b, pt: (b, 0, 0))],
        scratch_shapes=[pltpu.VMEM((2, DQ_COLS, tok), F32),
                        pltpu.VMEM((2, tok * N_DIFF_HEADS, DIFF_VDIM), F32),
                        pltpu.VMEM((2, tok, MLA_KV_RANK), F32),
                        pltpu.VMEM((2, LANES, tok), F32),
                        pltpu.SemaphoreType.DMA((4, 2)),
                        pltpu.VMEM((PAGE_SIZE, DQ_COLS), F32),
                        pltpu.VMEM((PAGE_SIZE, DQ_COLS), F32),
                        pltpu.VMEM((PAGE_SIZE, QCAT), F32)])
    return pl.pallas_call(
        functools.partial(_decode_kernel, n_pages=n_pages),
        grid_spec=grid_spec,
        out_shape=[jax.ShapeDtypeStruct((nb, N_DIFF_HEADS, n_new, DIFF_VDIM), F32),
                   jax.ShapeDtypeStruct((nb, n_rows, MLA_KV_RANK), F32)],
        compiler_params=pltpu.CompilerParams(dimension_semantics=("arbitrary",),
                                             vmem_limit_bytes=VMEM_LIMIT),
        name="decode_attn",
    )(page_table, lam, qd_blk, qm, k_self, v_self, kc_self, kt_pages, v_pages, c_pages, pe_pages)


def _out_kernel(x_ref, g_ref, od_ref, om_ref, sub_ref, wuv_ref, wo_ref, o_ref):
    parts = []
    for h in range(N_DIFF_HEADS):
        o = od_ref[0, :, DIFF_VDIM * h:DIFF_VDIM * (h + 1)]
        parts.append((_rms(o, sub_ref[...]) * (1.0 - LAMBDA_INIT)).astype(BF16))
    pair = 2 * MLA_KV_RANK
    for j in range(N_MLA_HEADS // 2):
        parts.append(_dot(om_ref[0, :, pair * j:pair * (j + 1)], wuv_ref[j]).astype(BF16))
    mixed = jnp.concatenate(parts, axis=1)
    o_ref[0] = x_ref[0] + g_ref[0] * _dot(mixed, wo_ref[...])


def _out_proj(x, mod, od, om, diff_subln, w_uv_pairs, w_o, *, tm):
    nb, t, _ = x.shape
    r = mod.shape[1]
    rb = 1 if r == 1 else tm
    gate_idx = (lambda b, i: (b, 0, 2)) if r == 1 else (lambda b, i: (b, i, 2))
    tok = lambda w: pl.BlockSpec((1, tm, w), lambda b, i: (b, i, 0))
    return pl.pallas_call(
        _out_kernel,
        grid=(nb, t // tm),
        in_specs=[tok(D_MODEL), pl.BlockSpec((1, rb, D_MODEL), gate_idx),
                  tok(DIFF_WIDTH), tok(N_MLA_HEADS * MLA_KV_RANK),
                  _const_spec((1, DIFF_VDIM)),
                  _const_spec((N_MLA_HEADS // 2, 2 * MLA_KV_RANK, LANES)),
                  _const_spec((DIFF_WIDTH + MLA_WIDTH, D_MODEL))],
        out_specs=tok(D_MODEL),
        out_shape=jax.ShapeDtypeStruct((nb, t, D_MODEL), F32),
        compiler_params=pltpu.CompilerParams(dimension_semantics=("parallel", "parallel"),
                                             vmem_limit_bytes=VMEM_LIMIT),
        name="out_proj",
    )(x, mod, od, om, diff_subln, w_uv_pairs, w_o)


FF_CHUNK = 256


def _ffn_kernel(x_ref, sh_ref, sc_ref, g_ref, fn_ref, wgu_ref, wd_ref, final_ref, y_ref):
    x = x_ref[0]
    h = (_rms(x, fn_ref[...]) * (1.0 + sc_ref[0]) + sh_ref[0]).astype(BF16)
    acc = jnp.zeros(x.shape, F32)
    for c in range(D_FF // FF_CHUNK):
        lo = FF_CHUNK * c
        gate = _dot(h, wgu_ref[:, lo:lo + FF_CHUNK])
        up = _dot(h, wgu_ref[:, D_FF + lo:D_FF + lo + FF_CHUNK])
        act = (gate * jax.nn.sigmoid(gate) * up).astype(BF16)
        acc = acc + _dot(act, wd_ref[lo:lo + FF_CHUNK, :])
    y_ref[0] = _rms(x + g_ref[0] * acc, final_ref[...])


def _ffn(x, mod, ffn_norm, w_gu, w_down, final_norm, *, tm):
    nb, t, _ = x.shape
    r = mod.shape[1]
    rb = 1 if r == 1 else tm
    mod_idx = (lambda col: (lambda b, i: (b, 0, col))) if r == 1 else (lambda col: (lambda b, i: (b, i, col)))
    tok = pl.BlockSpec((1, tm, D_MODEL), lambda b, i: (b, i, 0))
    return pl.pallas_call(
        _ffn_kernel,
        grid=(nb, t // tm),
        in_specs=[tok,
                  pl.BlockSpec((1, rb, D_MODEL), mod_idx(3)),
                  pl.BlockSpec((1, rb, D_MODEL), mod_idx(4)),
                  pl.BlockSpec((1, rb, D_MODEL), mod_idx(5)),
                  _const_spec((1, D_MODEL)),
                  _const_spec((D_MODEL, 2 * D_FF)),
                  _const_spec((D_FF, D_MODEL)),
                  _const_spec((1, D_MODEL))],
        out_specs=tok,
        out_shape=jax.ShapeDtypeStruct((nb, t, D_MODEL), F32),
        compiler_params=pltpu.CompilerParams(dimension_semantics=("parallel", "parallel"),
                                             vmem_limit_bytes=VMEM_LIMIT),
        name="ffn",
    )(x, mod, mod, mod, ffn_norm, w_gu, w_down, final_norm)


def _rope_tables(pos, rot, period):
    half = rot // 2
    inv = ROPE_THETA ** (-(jnp.arange(half, dtype=F32) * 2.0 / rot))
    ang = pos.astype(F32)[:, None] * inv[None, :]
    cos, sin = jnp.cos(ang), jnp.sin(ang)
    j = jnp.arange(LANES) % period
    first = j < half
    second = (j >= half) & (j < rot)
    idx = jnp.where(first, j, jnp.clip(j - half, 0, half - 1))
    c = jnp.where(first | second, cos[:, idx], 1.0)
    s1 = jnp.where(first, -sin[:, idx], 0.0)
    s2 = jnp.where(second, sin[:, idx], 0.0)
    return jnp.stack([c, s1, s2])


def _prep_weights(w_in, w_uq, w_ukv, w_o, w_gate_up, w_down):
    w_in_p = jnp.pad(w_in, ((0, 0), (0, IN_COLS_PAD - IN_COLS))).astype(BF16)
    uq = w_uq.reshape(MLA_Q_RANK, N_MLA_HEADS, MLA_NOPE + MLA_ROPE)
    w_uq_p = jnp.concatenate([uq[:, :, :MLA_NOPE].reshape(MLA_Q_RANK, -1),
                              uq[:, :, MLA_NOPE:].reshape(MLA_Q_RANK, -1)], axis=1).astype(BF16)
    ukv = w_ukv.reshape(MLA_KV_RANK, N_MLA_HEADS, MLA_NOPE + MLA_V)
    uk_t = jnp.transpose(ukv[:, :, :MLA_NOPE], (1, 2, 0))
    uv = jnp.transpose(ukv[:, :, MLA_NOPE:], (1, 0, 2))
    zk = jnp.zeros_like(uk_t[0])
    zv = jnp.zeros_like(uv[0])
    uk_pairs = jnp.stack([jnp.block([[uk_t[2 * j], zk], [zk, uk_t[2 * j + 1]]])
                          for j in range(N_MLA_HEADS // 2)]).astype(BF16)
    uv_pairs = jnp.stack([jnp.block([[uv[2 * j], zv], [zv, uv[2 * j + 1]]])
                          for j in range(N_MLA_HEADS // 2)]).astype(BF16)
    return w_in_p, w_uq_p, uk_pairs, uv_pairs, w_o.astype(BF16), w_gate_up.astype(BF16), w_down.astype(BF16)


def kernel(x_prompt, x_sample, c_prompt, c_sample, cache_k_diff, cache_v_diff, cache_ckv, cache_kpe, page_table, mod_w, mod_b, attn_norm, w_in, q_norm, kv_norm, w_uq, w_ukv, lambda_q1, lambda_k1, lambda_q2, lambda_k2, diff_subln, w_o, ffn_norm, w_gate_up, w_down, final_norm):
    assert mod_w.shape[0] == 1, "single-layer trunk"
    bp, tp, _ = x_prompt.shape
    bs, ts, _ = x_sample.shape
    n_pool = cache_ckv.shape[1]
    n_pages = page_table.shape[1]
    assert n_pages % PAGES_PER_CHUNK == 0
    ns = bs * ts

    w_in_p, w_uq_p, uk_pairs, uv_pairs, w_o_b, w_gu_b, w_down_b = _prep_weights(
        w_in[0], w_uq[0], w_ukv[0], w_o[0], w_gate_up[0], w_down[0])
    final_norm2 = final_norm.reshape(1, D_MODEL)

    lam_vecs = jnp.concatenate([lambda_q1, lambda_k1, lambda_q2, lambda_k2], axis=0)
    mod, lam = _modulation(jnp.concatenate([c_prompt, c_sample], axis=0), mod_w[0], mod_b, lam_vecs)
    mod_p = mod[:bp].reshape(bp, 1, 6 * D_MODEL)
    mod_s = jnp.repeat(mod[bp:], ts, axis=0).reshape(1, ns, 6 * D_MODEL)

    pos_p = jnp.arange(tp)
    pos_s = n_pages * PAGE_SIZE + (jnp.arange(ns) % ts)
    shared = (attn_norm, w_in_p, q_norm, kv_norm, w_uq_p, uk_pairs)

    tm_p = min(256, tp)
    dq, dk_f, dk_b, dv_f, dv_b, ckv_f, kpe_f, kcat, qcat = _in_proj(
        x_prompt, mod_p, _rope_tables(pos_p, DIFF_ROT, DIFF_DH), _rope_tables(pos_p, MLA_ROPE, MLA_ROPE),
        *shared, tm=tm_p)
    od_p, om_p = _prompt_attn(lam, dq, qcat, dk_b, dv_b, kcat, tq=tm_p)
    tm_f = min(512, tp)
    x1_p = _out_proj(x_prompt, mod_p, od_p, om_p, diff_subln, uv_pairs, w_o_b, tm=tm_f)
    y_prompt = _ffn(x1_p, mod_p, ffn_norm, w_gu_b, w_down_b, final_norm2, tm=tm_f)

    xs = x_sample.reshape(1, ns, D_MODEL)
    tm_s = min(256, ns)
    sq, sk_f, _, sv_f, _, sckv_f, skpe_f, skcat, sqcat = _in_proj(
        xs, mod_s, _rope_tables(pos_s, DIFF_ROT, DIFF_DH), _rope_tables(pos_s, MLA_ROPE, MLA_ROPE),
        *shared, tm=tm_s)
    n_hm = 2 * N_DIFF_HEADS
    sq5 = jnp.transpose(sq.reshape(bs, ts, n_hm, DIFF_DH), (0, 2, 1, 3))
    qd_blk = (sq5[:, :, :, None, :] * jnp.eye(n_hm, dtype=BF16)[None, :, None, :, None]
              ).reshape(bs, n_hm * ts, DQ_COLS)
    qm = jnp.transpose(sqcat.reshape(N_MLA_HEADS, bs, ts, QCAT), (1, 0, 2, 3)).reshape(bs, N_MLA_HEADS * ts, QCAT)
    kt_pages = jnp.transpose(cache_k_diff[0], (0, 2, 3, 4, 1)).reshape(n_pool, DQ_COLS, PAGE_SIZE)
    v_pages = cache_v_diff[0].reshape(n_pool, PAGE_SIZE * N_DIFF_HEADS, DIFF_VDIM)
    pe_pages = jnp.transpose(cache_kpe[0], (0, 2, 1))
    od_s, om_s = _decode_attn(page_table, lam, qd_blk, qm,
                              sk_f.reshape(bs, ts, DQ_COLS), sv_f.reshape(bs, ts, DQ_COLS),
                              skcat.reshape(bs, ts, QCAT), kt_pages, v_pages, cache_ckv[0], pe_pages)
    od_s = jnp.transpose(od_s, (0, 2, 1, 3)).reshape(1, ns, DIFF_WIDTH)
    om_s = jnp.transpose(om_s.reshape(bs, N_MLA_HEADS, ts, MLA_KV_RANK), (0, 2, 1, 3)
                         ).reshape(1, ns, N_MLA_HEADS * MLA_KV_RANK).astype(BF16)
    tm_fs = min(512, ns)
    x1_s = _out_proj(xs, mod_s, od_s, om_s, diff_subln, uv_pairs, w_o_b, tm=tm_fs)
    y_sample = _ffn(x1_s, mod_s, ffn_norm, w_gu_b, w_down_b, final_norm2, tm=tm_fs).reshape(bs, ts, D_MODEL)

    return (y_prompt, y_sample,
            dk_f.reshape(1, bp, tp, N_DIFF_HEADS, 2, DIFF_DH),
            dv_f.reshape(1, bp, tp, N_DIFF_HEADS, DIFF_VDIM),
            ckv_f.reshape(1, bp, tp, MLA_KV_RANK),
            kpe_f.reshape(1, bp, tp, MLA_ROPE),
            sk_f.reshape(1, bs, ts, N_DIFF_HEADS, 2, DIFF_DH),
            sv_f.reshape(1, bs, ts, N_DIFF_HEADS, DIFF_VDIM),
            sckv_f.reshape(1, bs, ts, MLA_KV_RANK),
            skpe_f.reshape(1, bs, ts, MLA_ROPE))
```

```python
import functools

import jax
import jax.numpy as jnp
import numpy as np
from jax import lax
from jax.experimental import pallas as pl
from jax.experimental.pallas import tpu as pltpu

F32 = jnp.float32
BF16 = jnp.bfloat16

D_MODEL = 1024
N_DIFF_HEADS = 4
DIFF_DH = 64
DIFF_VDIM = 2 * DIFF_DH
DIFF_ROT = DIFF_DH // 4
N_MLA_HEADS = 8
MLA_NOPE = 64
MLA_ROPE = 32
MLA_V = 64
MLA_Q_RANK = 384
MLA_KV_RANK = 256
DIFF_WIDTH = N_DIFF_HEADS * DIFF_VDIM
MLA_WIDTH = N_MLA_HEADS * MLA_V
DQ_COLS = N_DIFF_HEADS * 2 * DIFF_DH
IN_COLS = 3 * DQ_COLS + MLA_Q_RANK + MLA_KV_RANK + MLA_ROPE
D_FF = 2816
ROPE_THETA = 500000.0
NORM_EPS = 1e-6
PAGE_SIZE = 128
LOG2E = 1.4426950408889634
DIFF_SCALE = DIFF_DH ** -0.5 * LOG2E
MLA_SCALE = (MLA_NOPE + MLA_ROPE) ** -0.5 * LOG2E
LAMBDA_INIT = 0.8 - 0.6

LANES = 128
IN_COLS_PAD = 2304
QCAT = MLA_KV_RANK + LANES
PAGES_PER_CHUNK = 8
DECODE_SLOTS = 4
VMEM_LIMIT = 56 * 1024 * 1024


def _nt_dot(a, b):
    return lax.dot_general(a, b, (((1,), (1,)), ((), ())), preferred_element_type=F32)


def _dot(a, b):
    return jnp.dot(a, b, preferred_element_type=F32)


def _rms(x, g):
    return x * lax.rsqrt(jnp.mean(x * x, axis=-1, keepdims=True) + NORM_EPS) * g


def _rope(x, tab_ref, half):
    return (x * tab_ref[0]
            + pltpu.roll(x, LANES - half, 1) * tab_ref[1]
            + pltpu.roll(x, half, 1) * tab_ref[2])


def _const_spec(shape):
    nd = len(shape)
    return pl.BlockSpec(shape, lambda *_: (0,) * nd, pipeline_mode=pl.Buffered(1))


def _mod_kernel(cp_ref, cs_ref, w_ref, b_ref, lv_ref, mp_ref, ms_ref, lam_ref):
    w = w_ref[...].astype(BF16)
    for c_ref, m_ref in ((cp_ref, mp_ref), (cs_ref, ms_ref)):
        c = c_ref[...]
        m_ref[...] = _dot((c * jax.nn.sigmoid(c)).astype(BF16), w) + b_ref[...]
    lv = lv_ref[...]
    s1 = jnp.sum(lv[0:1] * lv[1:2], axis=-1, keepdims=True)
    s2 = jnp.sum(lv[2:3] * lv[3:4], axis=-1, keepdims=True)
    lam_ref[...] = jnp.broadcast_to(jnp.exp(s1) - jnp.exp(s2) + LAMBDA_INIT, lam_ref.shape)


def _modulation(c_p, c_s, mod_w, mod_b, lam_vecs):
    n_p, n_s = c_p.shape[0], c_s.shape[0]
    tn = 1024
    return pl.pallas_call(
        _mod_kernel,
        grid=(6 * D_MODEL // tn,),
        in_specs=[pl.BlockSpec((n_p, D_MODEL), lambda j: (0, 0)),
                  pl.BlockSpec((n_s, D_MODEL), lambda j: (0, 0)),
                  pl.BlockSpec((D_MODEL, tn), lambda j: (0, j)),
                  pl.BlockSpec((1, tn), lambda j: (0, j)),
                  pl.BlockSpec((4, DIFF_DH), lambda j: (0, 0))],
        out_specs=[pl.BlockSpec((n_p, tn), lambda j: (0, j)),
                   pl.BlockSpec((n_s, tn), lambda j: (0, j)),
                   pl.BlockSpec((1, LANES), lambda j: (0, 0))],
        out_shape=[jax.ShapeDtypeStruct((n_p, 6 * D_MODEL), F32),
                   jax.ShapeDtypeStruct((n_s, 6 * D_MODEL), F32),
                   jax.ShapeDtypeStruct((1, LANES), F32)],
        compiler_params=pltpu.CompilerParams(dimension_semantics=("arbitrary",)),
        name="modulation",
    )(c_p, c_s, mod_w, mod_b, lam_vecs)


def _in_kernel(x_ref, sh_ref, sc_ref, rd_ref, rp_ref, an_ref, win_ref, qn_ref, kn_ref, wuq_ref, wuk_ref,
               dq_ref, dkf_ref, dkb_ref, dvf_ref, dvb_ref, ckvf_ref, kpef_ref, kcat_ref, qcat_ref):
    x = x_ref[0]
    tm = x.shape[0]
    h = (_rms(x, an_ref[...]) * (1.0 + sc_ref[0]) + sh_ref[0]).astype(BF16)
    z = _dot(h, win_ref[...])
    for j in range(DQ_COLS // LANES):
        sl = slice(LANES * j, LANES * (j + 1))
        dq_ref[0, :, sl] = (_rope(z[:, sl], rd_ref, DIFF_ROT // 2) * DIFF_SCALE).astype(BF16)
        k = _rope(z[:, DQ_COLS + LANES * j:DQ_COLS + LANES * (j + 1)], rd_ref, DIFF_ROT // 2)
        dkf_ref[0, :, sl] = k
        dkb_ref[0, :, sl] = k.astype(BF16)
    dv = z[:, 2 * DQ_COLS:3 * DQ_COLS]
    for hh in range(N_DIFF_HEADS):
        dvf_ref[0, pl.ds(hh, tm, stride=N_DIFF_HEADS), :] = dv[:, DIFF_VDIM * hh:DIFF_VDIM * (hh + 1)]
    dvb_ref[0] = dv.astype(BF16)

    c0 = 3 * DQ_COLS
    cqn = _rms(z[:, c0:c0 + MLA_Q_RANK], qn_ref[...]).astype(BF16)
    q = _dot(cqn, wuq_ref[...])
    nope_w = N_MLA_HEADS * MLA_NOPE
    qn = q[:, :nope_w].astype(BF16)
    for j in range(N_MLA_HEADS // 2):
        ql = _dot(qn[:, LANES * j:LANES * (j + 1)], wuk_ref[j])
        qcat_ref[0, 2 * j, :, 0:MLA_KV_RANK] = (ql[:, :MLA_KV_RANK] * MLA_SCALE).astype(BF16)
        qcat_ref[0, 2 * j + 1, :, 0:MLA_KV_RANK] = (ql[:, MLA_KV_RANK:] * MLA_SCALE).astype(BF16)
    lane = lax.broadcasted_iota(jnp.int32, (tm, LANES), 1)
    heads_per_slab = LANES // MLA_ROPE
    for g in range(N_MLA_HEADS // heads_per_slab):
        qp = _rope(q[:, nope_w + LANES * g:nope_w + LANES * (g + 1)], rp_ref, MLA_ROPE // 2) * MLA_SCALE
        for i in range(heads_per_slab):
            moved = qp if i == 0 else pltpu.roll(qp, LANES - MLA_ROPE * i, 1)
            qcat_ref[0, heads_per_slab * g + i, :, MLA_KV_RANK:QCAT] = (
                jnp.where(lane < MLA_ROPE, moved, 0.0).astype(BF16))

    c1 = c0 + MLA_Q_RANK
    ckvn = _rms(z[:, c1:c1 + MLA_KV_RANK], kn_ref[...])
    ckvf_ref[0] = ckvn
    kpe = _rope(z[:, c1 + MLA_KV_RANK:IN_COLS_PAD], rp_ref, MLA_ROPE // 2)
    kpef_ref[0] = kpe[:, :MLA_ROPE]
    kcat_ref[0, :, 0:MLA_KV_RANK] = ckvn.astype(BF16)
    kcat_ref[0, :, MLA_KV_RANK:QCAT] = kpe.astype(BF16)


def _in_proj(x, mod, rope_d, rope_p, attn_norm, w_in_p, q_norm, kv_norm, w_uq_p, w_uk_pairs, *, tm):
    nb, t, _ = x.shape
    r = mod.shape[1]
    rb = 1 if r == 1 else tm
    mod_idx = (lambda col: (lambda b, i: (b, 0, col))) if r == 1 else (lambda col: (lambda b, i: (b, i, col)))
    tok = lambda w: pl.BlockSpec((1, tm, w), lambda b, i: (b, i, 0))
    rope_spec = pl.BlockSpec((3, tm, LANES), lambda b, i: (0, i, 0))
    out_shape = [jax.ShapeDtypeStruct((nb, t, DQ_COLS), BF16),
                 jax.ShapeDtypeStruct((nb, t, DQ_COLS), F32), jax.ShapeDtypeStruct((nb, t, DQ_COLS), BF16),
                 jax.ShapeDtypeStruct((nb, t * N_DIFF_HEADS, DIFF_VDIM), F32),
                 jax.ShapeDtypeStruct((nb, t, DQ_COLS), BF16),
                 jax.ShapeDtypeStruct((nb, t, MLA_KV_RANK), F32),
                 jax.ShapeDtypeStruct((nb, t, MLA_ROPE), F32),
                 jax.ShapeDtypeStruct((nb, t, QCAT), BF16),
                 jax.ShapeDtypeStruct((nb, N_MLA_HEADS, t, QCAT), BF16)]
    out_specs = [tok(DQ_COLS), tok(DQ_COLS), tok(DQ_COLS),
                 pl.BlockSpec((1, tm * N_DIFF_HEADS, DIFF_VDIM), lambda b, i: (b, i, 0)), tok(DQ_COLS),
                 tok(MLA_KV_RANK), tok(MLA_ROPE), tok(QCAT),
                 pl.BlockSpec((1, N_MLA_HEADS, tm, QCAT), lambda b, i: (b, 0, i, 0))]
    return pl.pallas_call(
        _in_kernel,
        grid=(nb, t // tm),
        in_specs=[tok(D_MODEL),
                  pl.BlockSpec((1, rb, D_MODEL), mod_idx(0)),
                  pl.BlockSpec((1, rb, D_MODEL), mod_idx(1)),
                  rope_spec, rope_spec,
                  _const_spec((1, D_MODEL)),
                  _const_spec((D_MODEL, IN_COLS_PAD)),
                  _const_spec((1, MLA_Q_RANK)),
                  _const_spec((1, MLA_KV_RANK)),
                  _const_spec((MLA_Q_RANK, N_MLA_HEADS * (MLA_NOPE + MLA_ROPE))),
                  _const_spec((N_MLA_HEADS // 2, LANES, 2 * MLA_KV_RANK))],
        out_specs=out_specs,
        out_shape=out_shape,
        compiler_params=pltpu.CompilerParams(dimension_semantics=("parallel", "parallel"),
                                             vmem_limit_bytes=VMEM_LIMIT),
        name="in_proj",
    )(x, mod, mod, rope_d, rope_p, attn_norm, w_in_p, q_norm, kv_norm, w_uq_p, w_uk_pairs)


def _online_update(s, m_ref, l_ref, acc_ref, v, idx):
    blocks = [s[:, LANES * j:LANES * (j + 1)] for j in range(s.shape[1] // LANES)]
    m_old = m_ref[idx]
    m_new = jnp.maximum(m_old, jnp.max(functools.reduce(jnp.maximum, blocks), axis=-1, keepdims=True))
    alpha = jnp.exp2(m_old - m_new)
    p = [jnp.exp2(blk - m_new) for blk in blocks]
    l_ref[idx] = alpha * l_ref[idx] + functools.reduce(jnp.add, p)
    pv = _dot(jnp.concatenate(p, axis=1).astype(BF16), v)
    acc = acc_ref[idx]
    scale = jnp.concatenate([alpha] * (acc.shape[1] // LANES), axis=1)
    acc_ref[idx] = scale * acc + pv
    m_ref[idx] = m_new


def _prompt_attn_kernel(lam_ref, dq_ref, qcat_ref, dk_ref, dv_ref, kcat_ref, od_ref, om_ref,
                        qd_s, md_s, ld_s, accd_s, mm_s, lm_s, accm_s, *, tq):
    qi = pl.program_id(1)
    lane = lax.broadcasted_iota(jnp.int32, (tq, LANES), 1)
    for h in range(N_DIFF_HEADS):
        qh = dq_ref[0, :, LANES * h:LANES * (h + 1)]
        qd_s[h, 0:tq, :] = jnp.where(lane < DIFF_DH, qh, jnp.zeros_like(qh))
        qd_s[h, tq:2 * tq, :] = jnp.where(lane >= DIFF_DH, qh, jnp.zeros_like(qh))
    md_s[...] = jnp.full(md_s.shape, -jnp.inf, F32)
    ld_s[...] = jnp.zeros(ld_s.shape, F32)
    accd_s[...] = jnp.zeros(accd_s.shape, F32)
    mm_s[...] = jnp.full(mm_s.shape, -jnp.inf, F32)
    lm_s[...] = jnp.zeros(lm_s.shape, F32)
    accm_s[...] = jnp.zeros(accm_s.shape, F32)

    def key_tile(kt, diagonal):
        k0 = pl.multiple_of(kt * tq, tq)

        def causal(s):
            if not diagonal:
                return s
            r = lax.broadcasted_iota(jnp.int32, s.shape, 0) % tq
            c = lax.broadcasted_iota(jnp.int32, s.shape, 1)
            return jnp.where(c <= r, s, -jnp.inf)

        for h in range(N_DIFF_HEADS):
            sl = slice(LANES * h, LANES * (h + 1))
            s = causal(_nt_dot(qd_s[h], dk_ref[0, pl.ds(k0, tq), sl]))
            _online_update(s, md_s, ld_s, accd_s, dv_ref[0, pl.ds(k0, tq), sl], h)
        kc = kcat_ref[0, pl.ds(k0, tq), :]
        s = causal(_nt_dot(qcat_ref[0].reshape(N_MLA_HEADS * tq, QCAT), kc))
        _online_update(s, mm_s, lm_s, accm_s, kc[:, :MLA_KV_RANK], slice(None))

    def body(kt, carry):
        key_tile(kt, False)
        return carry

    lax.fori_loop(0, qi, body, 0)
    key_tile(qi, True)

    lam = lam_ref[...]
    for h in range(N_DIFF_HEADS):
        o = accd_s[h] / jnp.sum(ld_s[h], axis=-1, keepdims=True)
        od_ref[0, :, LANES * h:LANES * (h + 1)] = o[:tq] - lam * o[tq:]
    om = accm_s[...] / jnp.sum(lm_s[...], axis=-1, keepdims=True)
    for h in range(N_MLA_HEADS):
        om_ref[0, :, MLA_KV_RANK * h:MLA_KV_RANK * (h + 1)] = om[h * tq:(h + 1) * tq].astype(BF16)


def _prompt_attn(lam, dq, qcat, dk, dv, kcat, *, tq):
    b, t, _ = dq.shape
    kv = lambda w: pl.BlockSpec((1, t, w), lambda bi, qi: (bi, 0, 0))
    return pl.pallas_call(
        functools.partial(_prompt_attn_kernel, tq=tq),
        grid=(b, t // tq),
        in_specs=[pl.BlockSpec((1, LANES), lambda bi, qi: (0, 0)),
                  pl.BlockSpec((1, tq, DQ_COLS), lambda bi, qi: (bi, qi, 0)),
                  pl.BlockSpec((1, N_MLA_HEADS, tq, QCAT), lambda bi, qi: (bi, 0, qi, 0)),
                  kv(DQ_COLS), kv(DQ_COLS), kv(QCAT)],
        out_specs=[pl.BlockSpec((1, tq, DIFF_WIDTH), lambda bi, qi: (bi, qi, 0)),
                   pl.BlockSpec((1, tq, N_MLA_HEADS * MLA_KV_RANK), lambda bi, qi: (bi, qi, 0))],
        out_shape=[jax.ShapeDtypeStruct((b, t, DIFF_WIDTH), F32),
                   jax.ShapeDtypeStruct((b, t, N_MLA_HEADS * MLA_KV_RANK), BF16)],
        scratch_shapes=[pltpu.VMEM((N_DIFF_HEADS, 2 * tq, LANES), BF16),
                        pltpu.VMEM((N_DIFF_HEADS, 2 * tq, LANES), F32),
                        pltpu.VMEM((N_DIFF_HEADS, 2 * tq, LANES), F32),
                        pltpu.VMEM((N_DIFF_HEADS, 2 * tq, DIFF_VDIM), F32),
                        pltpu.VMEM((N_MLA_HEADS * tq, LANES), F32),
                        pltpu.VMEM((N_MLA_HEADS * tq, LANES), F32),
                        pltpu.VMEM((N_MLA_HEADS * tq, MLA_KV_RANK), F32)],
        compiler_params=pltpu.CompilerParams(dimension_semantics=("parallel", "arbitrary"),
                                             vmem_limit_bytes=VMEM_LIMIT),
        name="prompt_attn",
    )(lam, dq, qcat, dk, dv, kcat)


def _decode_kernel(pt_ref, lam_ref, qd_ref, qm_ref, ks_ref, vs_ref, cs_ref,
                   kt_hbm, v_hbm, c_hbm, pe_hbm, od_ref, om_ref,
                   kbuf, vbuf, cbuf, pbuf, sems, kpad, vpad, cpad, *, n_pages):
    ch = PAGES_PER_CHUNK
    n_chunks = n_pages // ch
    b = pl.program_id(0)
    nb = pl.num_programs(0)
    tok = ch * PAGE_SIZE
    n_rows = qd_ref.shape[1]
    heads_rows = n_rows // N_DIFF_HEADS

    def copies(bb, c, slot):
        out = []
        for j in range(ch):
            pg = pt_ref[bb, c * ch + j]
            out.append(pltpu.make_async_copy(kt_hbm.at[pg], kbuf.at[slot, :, pl.ds(j * PAGE_SIZE, PAGE_SIZE)],
                                             sems.at[0, slot]))
            out.append(pltpu.make_async_copy(v_hbm.at[pg], vbuf.at[slot, pl.ds(j * PAGE_SIZE * N_DIFF_HEADS,
                                                                              PAGE_SIZE * N_DIFF_HEADS), :],
                                             sems.at[1, slot]))
            out.append(pltpu.make_async_copy(c_hbm.at[pg], cbuf.at[slot, pl.ds(j * PAGE_SIZE, PAGE_SIZE), :],
                                             sems.at[2, slot]))
            out.append(pltpu.make_async_copy(pe_hbm.at[pg],
                                             pbuf.at[slot, pl.ds(0, MLA_ROPE), pl.ds(j * PAGE_SIZE, PAGE_SIZE)],
                                             sems.at[3, slot]))
        return out

    def start(cps):
        for i, cp in enumerate(cps):
            cp.start(priority=i % 2)

    @pl.when(b == 0)
    def _():
        pbuf[...] = jnp.zeros(pbuf.shape, F32)
        kpad[...] = jnp.zeros(kpad.shape, F32)
        vpad[...] = jnp.zeros(vpad.shape, F32)
        cpad[...] = jnp.zeros(cpad.shape, F32)
        for d in range(DECODE_SLOTS - 1):
            start(copies(d // n_chunks, d % n_chunks, d))

    qd = qd_ref[0]
    qm = qm_ref[0]
    qm_lat = qm[:, :MLA_KV_RANK]
    qm_pe = qm[:, MLA_KV_RANK:]

    def softmax_step(s, m, l):
        m_new = jnp.maximum(m, jnp.max(s, axis=-1, keepdims=True))
        alpha = jnp.exp2(m - m_new)
        p = jnp.exp2(s - m_new)
        return p.astype(BF16), alpha, m_new, alpha * l + jnp.sum(p, axis=-1, keepdims=True)

    def chunk_body(c, carry):
        md, ld, accd, mm, lm, accm = carry
        g = b * n_chunks + c
        slot = lax.rem(g, DECODE_SLOTS)
        ahead = g + (DECODE_SLOTS - 1)

        @pl.when(ahead < nb * n_chunks)
        def _():
            start(copies(lax.div(ahead, n_chunks), lax.rem(ahead, n_chunks), lax.rem(ahead, DECODE_SLOTS)))

        for cp in copies(b, c, slot):
            cp.wait()

        s_d = _dot(qd, kbuf[slot].astype(BF16))
        cb = cbuf[slot].astype(BF16)
        s_m = _nt_dot(qm_lat, cb) + _dot(qm_pe, pbuf[slot].astype(BF16))
        p_d, a_d, md, ld = softmax_step(s_d, md, ld)
        p_m, a_m, mm, lm = softmax_step(s_m, mm, lm)
        new_accd = []
        for h in range(N_DIFF_HEADS):
            rows = slice(heads_rows * h, heads_rows * (h + 1))
            vh = vbuf[slot, pl.ds(h, tok, stride=N_DIFF_HEADS), :].astype(BF16)
            new_accd.append(a_d[rows] * accd[h] + _dot(p_d, vh)[rows])
        accm = a_m * accm + _dot(p_m, cb)
        return md, ld, tuple(new_accd), mm, lm, accm

    init = (jnp.full((n_rows, 1), -jnp.inf, F32), jnp.zeros((n_rows, 1), F32),
            tuple(jnp.zeros((heads_rows, DIFF_VDIM), F32) for _ in range(N_DIFF_HEADS)),
            jnp.full((n_rows, 1), -jnp.inf, F32), jnp.zeros((n_rows, 1), F32),
            jnp.zeros((n_rows, MLA_KV_RANK), F32))
    md, ld, accd, mm, lm, accm = lax.fori_loop(0, n_chunks, chunk_body, init)

    n_new = ks_ref.shape[1]
    kpad[0:n_new, :] = ks_ref[0]
    vpad[0:n_new, :] = vs_ref[0]
    cpad[0:n_new, :] = cs_ref[0].astype(F32)
    qpos = lax.broadcasted_iota(jnp.int32, (n_rows, PAGE_SIZE), 0) % n_new
    kpos = lax.broadcasted_iota(jnp.int32, (n_rows, PAGE_SIZE), 1)
    ok = kpos <= qpos
    cself = cpad[...].astype(BF16)
    s_d = jnp.where(ok, _nt_dot(qd, kpad[...].astype(BF16)), -jnp.inf)
    s_m = jnp.where(ok, _nt_dot(qm, cself), -jnp.inf)
    p_d, a_d, md, ld = softmax_step(s_d, md, ld)
    p_m, a_m, mm, lm = softmax_step(s_m, mm, lm)
    pv = _dot(p_d, vpad[...].astype(BF16))
    accm = a_m * accm + _dot(p_m, cself[:, :MLA_KV_RANK])

    lam = lam_ref[...]
    half = heads_rows // 2
    for h in range(N_DIFF_HEADS):
        rows = slice(heads_rows * h, heads_rows * (h + 1))
        acc = a_d[rows] * accd[h] + pv[rows, DIFF_VDIM * h:DIFF_VDIM * (h + 1)]
        o = acc / ld[rows]
        od_ref[0, h] = o[:half] - lam * o[half:]
    om_ref[0] = accm / lm


def _decode_attn(page_table, lam, qd_blk, qm, k_self, v_self, kc_self, kt_pages, v_pages, c_pages, pe_pages):
    nb, n_pages = page_table.shape
    n_rows = qd_blk.shape[1]
    n_new = k_self.shape[1]
    ch = PAGES_PER_CHUNK
    tok = ch * PAGE_SIZE
    row_blk = lambda w: pl.BlockSpec((1, n_rows, w), lambda b, pt: (b, 0, 0))
    new_blk = lambda w: pl.BlockSpec((1, n_new, w), lambda b, pt: (b, 0, 0))
    any_spec = pl.BlockSpec(memory_space=pl.ANY)
    grid_spec = pltpu.PrefetchScalarGridSpec(
        num_scalar_prefetch=1,
        grid=(nb,),
        in_specs=[pl.BlockSpec((1, LANES), lambda b, pt: (0, 0)),
                  row_blk(DQ_COLS), row_blk(QCAT), new_blk(DQ_COLS), new_blk(DQ_COLS), new_blk(QCAT),
                  any_spec, any_spec, any_spec, any_spec],
        out_specs=[pl.BlockSpec((1, N_DIFF_HEADS, n_new, DIFF_VDIM), lambda b, pt: (b, 0, 0, 0)),
                   pl.BlockSpec((1, n_rows, MLA_KV_RANK), lambda b, pt: (b, 0, 0))],
        scratch_shapes=[pltpu.VMEM((DECODE_SLOTS, DQ_COLS, tok), F32),
                        pltpu.VMEM((DECODE_SLOTS, tok * N_DIFF_HEADS, DIFF_VDIM), F32),
                        pltpu.VMEM((DECODE_SLOTS, tok, MLA_KV_RANK), F32),
                        pltpu.VMEM((DECODE_SLOTS, LANES, tok), F32),
                        pltpu.SemaphoreType.DMA((4, DECODE_SLOTS)),
                        pltpu.VMEM((PAGE_SIZE, DQ_COLS), F32),
                        pltpu.VMEM((PAGE_SIZE, DQ_COLS), F32),
                        pltpu.VMEM((PAGE_SIZE, QCAT), F32)])
    return pl.pallas_call(
        functools.partial(_decode_kernel, n_pages=n_pages),
        grid_spec=grid_spec,
        out_shape=[jax.ShapeDtypeStruct((nb, N_DIFF_HEADS, n_new, DIFF_VDIM), F32),
                   jax.ShapeDtypeStruct((nb, n_rows, MLA_KV_RANK), F32)],
        compiler_params=pltpu.CompilerParams(dimension_semantics=("arbitrary",),
                                             vmem_limit_bytes=VMEM_LIMIT),
        name="decode_attn",
    )(page_table, lam, qd_blk, qm, k_self, v_self, kc_self, kt_pages, v_pages, c_pages, pe_pages)


def _out_kernel(x_ref, g_ref, od_ref, om_ref, sub_ref, wuv_ref, wo_ref, o_ref):
    parts = []
    for h in range(N_DIFF_HEADS):
        o = od_ref[0, :, DIFF_VDIM * h:DIFF_VDIM * (h + 1)]
        parts.append((_rms(o, sub_ref[...]) * (1.0 - LAMBDA_INIT)).astype(BF16))
    pair = 2 * MLA_KV_RANK
    for j in range(N_MLA_HEADS // 2):
        parts.append(_dot(om_ref[0, :, pair * j:pair * (j + 1)], wuv_ref[j]).astype(BF16))
    mixed = jnp.concatenate(parts, axis=1)
    o_ref[0] = x_ref[0] + g_ref[0] * _dot(mixed, wo_ref[...])


def _out_proj(x, mod, od, om, diff_subln, w_uv_pairs, w_o, *, tm):
    nb, t, _ = x.shape
    r = mod.shape[1]
    rb = 1 if r == 1 else tm
    gate_idx = (lambda b, i: (b, 0, 2)) if r == 1 else (lambda b, i: (b, i, 2))
    tok = lambda w: pl.BlockSpec((1, tm, w), lambda b, i: (b, i, 0))
    return pl.pallas_call(
        _out_kernel,
        grid=(nb, t // tm),
        in_specs=[tok(D_MODEL), pl.BlockSpec((1, rb, D_MODEL), gate_idx),
                  tok(DIFF_WIDTH), tok(N_MLA_HEADS * MLA_KV_RANK),
                  _const_spec((1, DIFF_VDIM)),
                  _const_spec((N_MLA_HEADS // 2, 2 * MLA_KV_RANK, LANES)),
                  _const_spec((DIFF_WIDTH + MLA_WIDTH, D_MODEL))],
        out_specs=tok(D_MODEL),
        out_shape=jax.ShapeDtypeStruct((nb, t, D_MODEL), F32),
        compiler_params=pltpu.CompilerParams(dimension_semantics=("parallel", "parallel"),
                                             vmem_limit_bytes=VMEM_LIMIT),
        name="out_proj",
    )(x, mod, od, om, diff_subln, w_uv_pairs, w_o)


FF_CHUNK = 256


def _ffn_kernel(x_ref, sh_ref, sc_ref, g_ref, fn_ref, wgu_ref, wd_ref, final_ref, y_ref):
    x = x_ref[0]
    h = (_rms(x, fn_ref[...]) * (1.0 + sc_ref[0]) + sh_ref[0]).astype(BF16)
    acc = jnp.zeros(x.shape, F32)
    for c in range(D_FF // FF_CHUNK):
        lo = FF_CHUNK * c
        gate = _dot(h, wgu_ref[:, lo:lo + FF_CHUNK])
        up = _dot(h, wgu_ref[:, D_FF + lo:D_FF + lo + FF_CHUNK])
        act = (gate * jax.nn.sigmoid(gate) * up).astype(BF16)
        acc = acc + _dot(act, wd_ref[lo:lo + FF_CHUNK, :])
    y_ref[0] = _rms(x + g_ref[0] * acc, final_ref[...])


def _ffn(x, mod, ffn_norm, w_gu, w_down, final_norm, *, tm):
    nb, t, _ = x.shape
    r = mod.shape[1]
    rb = 1 if r == 1 else tm
    mod_idx = (lambda col: (lambda b, i: (b, 0, col))) if r == 1 else (lambda col: (lambda b, i: (b, i, col)))
    tok = pl.BlockSpec((1, tm, D_MODEL), lambda b, i: (b, i, 0))
    return pl.pallas_call(
        _ffn_kernel,
        grid=(nb, t // tm),
        in_specs=[tok,
                  pl.BlockSpec((1, rb, D_MODEL), mod_idx(3)),
                  pl.BlockSpec((1, rb, D_MODEL), mod_idx(4)),
                  pl.BlockSpec((1, rb, D_MODEL), mod_idx(5)),
                  _const_spec((1, D_MODEL)),
                  _const_spec((D_MODEL, 2 * D_FF)),
                  _const_spec((D_FF, D_MODEL)),
                  _const_spec((1, D_MODEL))],
        out_specs=tok,
        out_shape=jax.ShapeDtypeStruct((nb, t, D_MODEL), F32),
        compiler_params=pltpu.CompilerParams(dimension_semantics=("parallel", "parallel"),
                                             vmem_limit_bytes=VMEM_LIMIT),
        name="ffn",
    )(x, mod, mod, mod, ffn_norm, w_gu, w_down, final_norm)


def _rope_tables(pos, rot, period):
    half = rot // 2
    j = np.arange(LANES) % period
    first = j < half
    second = (j >= half) & (j < rot)
    freq_idx = np.where(first, j, np.clip(j - half, 0, half - 1)).astype(np.float32)
    inv = ROPE_THETA ** (-(jnp.asarray(freq_idx) * 2.0 / rot))
    ang = pos.astype(F32)[:, None] * inv[None, :]
    cos, sin = jnp.cos(ang), jnp.sin(ang)
    return jnp.stack([jnp.where(first | second, cos, 1.0),
                      jnp.where(first, -sin, 0.0),
                      jnp.where(second, sin, 0.0)])


def _prep_weights(w_in, w_uq, w_ukv, w_o, w_gate_up, w_down):
    w_in_p = jnp.pad(w_in, ((0, 0), (0, IN_COLS_PAD - IN_COLS))).astype(BF16)
    uq = w_uq.reshape(MLA_Q_RANK, N_MLA_HEADS, MLA_NOPE + MLA_ROPE)
    w_uq_p = jnp.concatenate([uq[:, :, :MLA_NOPE].reshape(MLA_Q_RANK, -1),
                              uq[:, :, MLA_NOPE:].reshape(MLA_Q_RANK, -1)], axis=1).astype(BF16)
    ukv = w_ukv.reshape(MLA_KV_RANK, N_MLA_HEADS, MLA_NOPE + MLA_V)
    uk_t = jnp.transpose(ukv[:, :, :MLA_NOPE], (1, 2, 0))
    uv = jnp.transpose(ukv[:, :, MLA_NOPE:], (1, 0, 2))
    zk = jnp.zeros_like(uk_t[0])
    zv = jnp.zeros_like(uv[0])
    uk_pairs = jnp.stack([jnp.block([[uk_t[2 * j], zk], [zk, uk_t[2 * j + 1]]])
                          for j in range(N_MLA_HEADS // 2)]).astype(BF16)
    uv_pairs = jnp.stack([jnp.block([[uv[2 * j], zv], [zv, uv[2 * j + 1]]])
                          for j in range(N_MLA_HEADS // 2)]).astype(BF16)
    return w_in_p, w_uq_p, uk_pairs, uv_pairs, w_o.astype(BF16), w_gate_up.astype(BF16), w_down.astype(BF16)


def kernel(x_prompt, x_sample, c_prompt, c_sample, cache_k_diff, cache_v_diff, cache_ckv, cache_kpe, page_table, mod_w, mod_b, attn_norm, w_in, q_norm, kv_norm, w_uq, w_ukv, lambda_q1, lambda_k1, lambda_q2, lambda_k2, diff_subln, w_o, ffn_norm, w_gate_up, w_down, final_norm):
    assert mod_w.shape[0] == 1, "single-layer trunk"
    bp, tp, _ = x_prompt.shape
    bs, ts, _ = x_sample.shape
    n_pool = cache_ckv.shape[1]
    n_pages = page_table.shape[1]
    assert n_pages % PAGES_PER_CHUNK == 0 and bs * (n_pages // PAGES_PER_CHUNK) >= DECODE_SLOTS - 1
    ns = bs * ts

    w_in_p, w_uq_p, uk_pairs, uv_pairs, w_o_b, w_gu_b, w_down_b = _prep_weights(
        w_in[0], w_uq[0], w_ukv[0], w_o[0], w_gate_up[0], w_down[0])
    final_norm2 = final_norm.reshape(1, D_MODEL)

    lam_vecs = jnp.concatenate([lambda_q1, lambda_k1, lambda_q2, lambda_k2], axis=0)
    mod_p, mod_s, lam = _modulation(c_prompt, jnp.repeat(c_sample, ts, axis=0), mod_w[0], mod_b, lam_vecs)
    mod_p = mod_p.reshape(bp, 1, 6 * D_MODEL)
    mod_s = mod_s.reshape(1, ns, 6 * D_MODEL)

    pos_p = jnp.arange(tp)
    pos_s = n_pages * PAGE_SIZE + (jnp.arange(ns) % ts)
    shared = (attn_norm, w_in_p, q_norm, kv_norm, w_uq_p, uk_pairs)

    tm_p = min(256, tp)
    dq, dk_f, dk_b, dv_f, dv_b, ckv_f, kpe_f, kcat, qcat = _in_proj(
        x_prompt, mod_p, _rope_tables(pos_p, DIFF_ROT, DIFF_DH), _rope_tables(pos_p, MLA_ROPE, MLA_ROPE),
        *shared, tm=tm_p)
    od_p, om_p = _prompt_attn(lam, dq, qcat, dk_b, dv_b, kcat, tq=tm_p)
    tm_f = min(512, tp)
    x1_p = _out_proj(x_prompt, mod_p, od_p, om_p, diff_subln, uv_pairs, w_o_b, tm=tm_f)
    y_prompt = _ffn(x1_p, mod_p, ffn_norm, w_gu_b, w_down_b, final_norm2, tm=tm_f)

    xs = x_sample.reshape(1, ns, D_MODEL)
    tm_s = min(256, ns)
    sq, sk_f, _, sv_f, _, sckv_f, skpe_f, skcat, sqcat = _in_proj(
        xs, mod_s, _rope_tables(pos_s, DIFF_ROT, DIFF_DH), _rope_tables(pos_s, MLA_ROPE, MLA_ROPE),
        *shared, tm=tm_s)
    n_hm = 2 * N_DIFF_HEADS
    sq5 = jnp.transpose(sq.reshape(bs, ts, n_hm, DIFF_DH), (0, 2, 1, 3))
    qd_blk = (sq5[:, :, :, None, :] * jnp.eye(n_hm, dtype=BF16)[None, :, None, :, None]
              ).reshape(bs, n_hm * ts, DQ_COLS)
    qm = jnp.transpose(sqcat.reshape(N_MLA_HEADS, bs, ts, QCAT), (1, 0, 2, 3)).reshape(bs, N_MLA_HEADS * ts, QCAT)
    kt_pages = jnp.transpose(cache_k_diff[0], (0, 2, 3, 4, 1)).reshape(n_pool, DQ_COLS, PAGE_SIZE)
    v_pages = cache_v_diff[0].reshape(n_pool, PAGE_SIZE * N_DIFF_HEADS, DIFF_VDIM)
    pe_pages = jnp.transpose(cache_kpe[0], (0, 2, 1))
    od_s, om_s = _decode_attn(page_table, lam, qd_blk, qm,
                              sk_f.reshape(bs, ts, DQ_COLS), sv_f.reshape(bs, ts, DQ_COLS),
                              skcat.reshape(bs, ts, QCAT), kt_pages, v_pages, cache_ckv[0], pe_pages)
    od_s = jnp.transpose(od_s, (0, 2, 1, 3)).reshape(1, ns, DIFF_WIDTH)
    om_s = jnp.transpose(om_s.reshape(bs, N_MLA_HEADS, ts, MLA_KV_RANK), (0, 2, 1, 3)
                         ).reshape(1, ns, N_MLA_HEADS * MLA_KV_RANK).astype(BF16)
    tm_fs = min(512, ns)
    x1_s = _out_proj(xs, mod_s, od_s, om_s, diff_subln, uv_pairs, w_o_b, tm=tm_fs)
    y_sample = _ffn(x1_s, mod_s, ffn_norm, w_gu_b, w_down_b, final_norm2, tm=tm_fs).reshape(bs, ts, D_MODEL)

    return (y_prompt, y_sample,
            dk_f.reshape(1, bp, tp, N_DIFF_HEADS, 2, DIFF_DH),
            dv_f.reshape(1, bp, tp, N_DIFF_HEADS, DIFF_VDIM),
            ckv_f.reshape(1, bp, tp, MLA_KV_RANK),
            kpe_f.reshape(1, bp, tp, MLA_ROPE),
            sk_f.reshape(1, bs, ts, N_DIFF_HEADS, 2, DIFF_DH),
            sv_f.reshape(1, bs, ts, N_DIFF_HEADS, DIFF_VDIM),
            sckv_f.reshape(1, bs, ts, MLA_KV_RANK),
            skpe_f.reshape(1, bs, ts, MLA_ROPE))
```

```python
import functools

import jax
import jax.numpy as jnp
import numpy as np
from jax import lax
from jax.experimental import pallas as pl
from jax.experimental.pallas import tpu as pltpu

F32 = jnp.float32
BF16 = jnp.bfloat16

D_MODEL = 1024
N_DIFF_HEADS = 4
DIFF_DH = 64
DIFF_VDIM = 2 * DIFF_DH
DIFF_ROT = DIFF_DH // 4
N_MLA_HEADS = 8
MLA_NOPE = 64
MLA_ROPE = 32
MLA_V = 64
MLA_Q_RANK = 384
MLA_KV_RANK = 256
DIFF_WIDTH = N_DIFF_HEADS * DIFF_VDIM
MLA_WIDTH = N_MLA_HEADS * MLA_V
DQ_COLS = N_DIFF_HEADS * 2 * DIFF_DH
IN_COLS = 3 * DQ_COLS + MLA_Q_RANK + MLA_KV_RANK + MLA_ROPE
D_FF = 2816
ROPE_THETA = 500000.0
NORM_EPS = 1e-6
PAGE_SIZE = 128
LOG2E = 1.4426950408889634
DIFF_SCALE = DIFF_DH ** -0.5 * LOG2E
MLA_SCALE = (MLA_NOPE + MLA_ROPE) ** -0.5 * LOG2E
LAMBDA_INIT = 0.8 - 0.6

LANES = 128
IN_COLS_PAD = 2304
QCAT = MLA_KV_RANK + LANES
PAGES_PER_CHUNK = 8
DECODE_SLOTS = 4
VMEM_LIMIT = 56 * 1024 * 1024


def _nt_dot(a, b):
    return lax.dot_general(a, b, (((1,), (1,)), ((), ())), preferred_element_type=F32)


def _dot(a, b):
    return jnp.dot(a, b, preferred_element_type=F32)


def _rms(x, g):
    return x * lax.rsqrt(jnp.mean(x * x, axis=-1, keepdims=True) + NORM_EPS) * g


def _rope(x, tab_ref, half):
    return (x * tab_ref[0]
            + pltpu.roll(x, LANES - half, 1) * tab_ref[1]
            + pltpu.roll(x, half, 1) * tab_ref[2])


def _const_spec(shape):
    nd = len(shape)
    return pl.BlockSpec(shape, lambda *_: (0,) * nd, pipeline_mode=pl.Buffered(1))


def _mod_kernel(cp_ref, cs_ref, w_ref, b_ref, lv_ref, mp_ref, ms_ref, lam_ref):
    w = w_ref[...].astype(BF16)
    for c_ref, m_ref in ((cp_ref, mp_ref), (cs_ref, ms_ref)):
        c = c_ref[...]
        m_ref[...] = _dot((c * jax.nn.sigmoid(c)).astype(BF16), w) + b_ref[...]
    lv = lv_ref[...]
    s1 = jnp.sum(lv[0:1] * lv[1:2], axis=-1, keepdims=True)
    s2 = jnp.sum(lv[2:3] * lv[3:4], axis=-1, keepdims=True)
    lam_ref[...] = jnp.broadcast_to(jnp.exp(s1) - jnp.exp(s2) + LAMBDA_INIT, lam_ref.shape)


def _modulation(c_p, c_s, mod_w, mod_b, lam_vecs):
    n_p, n_s = c_p.shape[0], c_s.shape[0]
    tn = 1024
    return pl.pallas_call(
        _mod_kernel,
        grid=(6 * D_MODEL // tn,),
        in_specs=[pl.BlockSpec((n_p, D_MODEL), lambda j: (0, 0)),
                  pl.BlockSpec((n_s, D_MODEL), lambda j: (0, 0)),
                  pl.BlockSpec((D_MODEL, tn), lambda j: (0, j)),
                  pl.BlockSpec((1, tn), lambda j: (0, j)),
                  pl.BlockSpec((4, DIFF_DH), lambda j: (0, 0))],
        out_specs=[pl.BlockSpec((n_p, tn), lambda j: (0, j)),
                   pl.BlockSpec((n_s, tn), lambda j: (0, j)),
                   pl.BlockSpec((1, LANES), lambda j: (0, 0))],
        out_shape=[jax.ShapeDtypeStruct((n_p, 6 * D_MODEL), F32),
                   jax.ShapeDtypeStruct((n_s, 6 * D_MODEL), F32),
                   jax.ShapeDtypeStruct((1, LANES), F32)],
        compiler_params=pltpu.CompilerParams(dimension_semantics=("arbitrary",)),
        name="modulation",
    )(c_p, c_s, mod_w, mod_b, lam_vecs)


def _in_kernel(x_ref, sh_ref, sc_ref, rd_ref, rp_ref, an_ref, win_ref, qn_ref, kn_ref, wuq_ref, wuk_ref,
               dq_ref, dkf_ref, dkb_ref, dvf_ref, dvb_ref, ckvf_ref, kpef_ref, kcat_ref, qcat_ref):
    x = x_ref[0]
    tm = x.shape[0]
    h = (_rms(x, an_ref[...]) * (1.0 + sc_ref[0]) + sh_ref[0]).astype(BF16)
    z = _dot(h, win_ref[...])
    for j in range(DQ_COLS // LANES):
        sl = slice(LANES * j, LANES * (j + 1))
        dq_ref[0, :, sl] = (_rope(z[:, sl], rd_ref, DIFF_ROT // 2) * DIFF_SCALE).astype(BF16)
        k = _rope(z[:, DQ_COLS + LANES * j:DQ_COLS + LANES * (j + 1)], rd_ref, DIFF_ROT // 2)
        dkf_ref[0, :, sl] = k
        dkb_ref[0, :, sl] = k.astype(BF16)
    dv = z[:, 2 * DQ_COLS:3 * DQ_COLS]
    for hh in range(N_DIFF_HEADS):
        dvf_ref[0, pl.ds(hh, tm, stride=N_DIFF_HEADS), :] = dv[:, DIFF_VDIM * hh:DIFF_VDIM * (hh + 1)]
    dvb_ref[0] = dv.astype(BF16)

    c0 = 3 * DQ_COLS
    cqn = _rms(z[:, c0:c0 + MLA_Q_RANK], qn_ref[...]).astype(BF16)
    q = _dot(cqn, wuq_ref[...])
    nope_w = N_MLA_HEADS * MLA_NOPE
    qn = q[:, :nope_w].astype(BF16)
    for j in range(N_MLA_HEADS // 2):
        ql = _dot(qn[:, LANES * j:LANES * (j + 1)], wuk_ref[j])
        qcat_ref[0, 2 * j, :, 0:MLA_KV_RANK] = (ql[:, :MLA_KV_RANK] * MLA_SCALE).astype(BF16)
        qcat_ref[0, 2 * j + 1, :, 0:MLA_KV_RANK] = (ql[:, MLA_KV_RANK:] * MLA_SCALE).astype(BF16)
    lane = lax.broadcasted_iota(jnp.int32, (tm, LANES), 1)
    heads_per_slab = LANES // MLA_ROPE
    for g in range(N_MLA_HEADS // heads_per_slab):
        qp = _rope(q[:, nope_w + LANES * g:nope_w + LANES * (g + 1)], rp_ref, MLA_ROPE // 2) * MLA_SCALE
        for i in range(heads_per_slab):
            moved = qp if i == 0 else pltpu.roll(qp, LANES - MLA_ROPE * i, 1)
            qcat_ref[0, heads_per_slab * g + i, :, MLA_KV_RANK:QCAT] = (
                jnp.where(lane < MLA_ROPE, moved, 0.0).astype(BF16))

    c1 = c0 + MLA_Q_RANK
    ckvn = _rms(z[:, c1:c1 + MLA_KV_RANK], kn_ref[...])
    ckvf_ref[0] = ckvn
    kpe = _rope(z[:, c1 + MLA_KV_RANK:IN_COLS_PAD], rp_ref, MLA_ROPE // 2)
    kpef_ref[0] = kpe[:, :MLA_ROPE]
    kcat_ref[0, :, 0:MLA_KV_RANK] = ckvn.astype(BF16)
    kcat_ref[0, :, MLA_KV_RANK:QCAT] = kpe.astype(BF16)


def _in_proj(x, mod, rope_d, rope_p, attn_norm, w_in_p, q_norm, kv_norm, w_uq_p, w_uk_pairs, *, tm):
    nb, t, _ = x.shape
    r = mod.shape[1]
    rb = 1 if r == 1 else tm
    mod_idx = (lambda col: (lambda b, i: (b, 0, col))) if r == 1 else (lambda col: (lambda b, i: (b, i, col)))
    tok = lambda w: pl.BlockSpec((1, tm, w), lambda b, i: (b, i, 0))
    rope_spec = pl.BlockSpec((3, tm, LANES), lambda b, i: (0, i, 0))
    out_shape = [jax.ShapeDtypeStruct((nb, t, DQ_COLS), BF16),
                 jax.ShapeDtypeStruct((nb, t, DQ_COLS), F32), jax.ShapeDtypeStruct((nb, t, DQ_COLS), BF16),
                 jax.ShapeDtypeStruct((nb, t * N_DIFF_HEADS, DIFF_VDIM), F32),
                 jax.ShapeDtypeStruct((nb, t, DQ_COLS), BF16),
                 jax.ShapeDtypeStruct((nb, t, MLA_KV_RANK), F32),
                 jax.ShapeDtypeStruct((nb, t, MLA_ROPE), F32),
                 jax.ShapeDtypeStruct((nb, t, QCAT), BF16),
                 jax.ShapeDtypeStruct((nb, N_MLA_HEADS, t, QCAT), BF16)]
    out_specs = [tok(DQ_COLS), tok(DQ_COLS), tok(DQ_COLS),
                 pl.BlockSpec((1, tm * N_DIFF_HEADS, DIFF_VDIM), lambda b, i: (b, i, 0)), tok(DQ_COLS),
                 tok(MLA_KV_RANK), tok(MLA_ROPE), tok(QCAT),
                 pl.BlockSpec((1, N_MLA_HEADS, tm, QCAT), lambda b, i: (b, 0, i, 0))]
    return pl.pallas_call(
        _in_kernel,
        grid=(nb, t // tm),
        in_specs=[tok(D_MODEL),
                  pl.BlockSpec((1, rb, D_MODEL), mod_idx(0)),
                  pl.BlockSpec((1, rb, D_MODEL), mod_idx(1)),
                  rope_spec, rope_spec,
                  _const_spec((1, D_MODEL)),
                  _const_spec((D_MODEL, IN_COLS_PAD)),
                  _const_spec((1, MLA_Q_RANK)),
                  _const_spec((1, MLA_KV_RANK)),
                  _const_spec((MLA_Q_RANK, N_MLA_HEADS * (MLA_NOPE + MLA_ROPE))),
                  _const_spec((N_MLA_HEADS // 2, LANES, 2 * MLA_KV_RANK))],
        out_specs=out_specs,
        out_shape=out_shape,
        compiler_params=pltpu.CompilerParams(dimension_semantics=("parallel", "parallel"),
                                             vmem_limit_bytes=VMEM_LIMIT),
        name="in_proj",
    )(x, mod, mod, rope_d, rope_p, attn_norm, w_in_p, q_norm, kv_norm, w_uq_p, w_uk_pairs)


def _online_update(s, m_ref, l_ref, acc_ref, v, idx):
    blocks = [s[:, LANES * j:LANES * (j + 1)] for j in range(s.shape[1] // LANES)]
    m_old = m_ref[idx]
    m_new = jnp.maximum(m_old, jnp.max(functools.reduce(jnp.maximum, blocks), axis=-1, keepdims=True))
    alpha = jnp.exp2(m_old - m_new)
    p = [jnp.exp2(blk - m_new) for blk in blocks]
    l_ref[idx] = alpha * l_ref[idx] + functools.reduce(jnp.add, p)
    pv = _dot(jnp.concatenate(p, axis=1).astype(BF16), v)
    acc = acc_ref[idx]
    scale = jnp.concatenate([alpha] * (acc.shape[1] // LANES), axis=1)
    acc_ref[idx] = scale * acc + pv
    m_ref[idx] = m_new


def _prompt_attn_kernel(lam_ref, dq_ref, qcat_ref, dk_ref, dv_ref, kcat_ref, od_ref, om_ref,
                        qd_s, md_s, ld_s, accd_s, mm_s, lm_s, accm_s, *, tq):
    qi = pl.program_id(1)
    lane = lax.broadcasted_iota(jnp.int32, (tq, LANES), 1)
    for h in range(N_DIFF_HEADS):
        qh = dq_ref[0, :, LANES * h:LANES * (h + 1)]
        qd_s[h, 0:tq, :] = jnp.where(lane < DIFF_DH, qh, jnp.zeros_like(qh))
        qd_s[h, tq:2 * tq, :] = jnp.where(lane >= DIFF_DH, qh, jnp.zeros_like(qh))
    md_s[...] = jnp.full(md_s.shape, -jnp.inf, F32)
    ld_s[...] = jnp.zeros(ld_s.shape, F32)
    accd_s[...] = jnp.zeros(accd_s.shape, F32)
    mm_s[...] = jnp.full(mm_s.shape, -jnp.inf, F32)
    lm_s[...] = jnp.zeros(lm_s.shape, F32)
    accm_s[...] = jnp.zeros(accm_s.shape, F32)

    def key_tile(kt, diagonal):
        k0 = pl.multiple_of(kt * tq, tq)

        def causal(s):
            if not diagonal:
                return s
            r = lax.broadcasted_iota(jnp.int32, s.shape, 0) % tq
            c = lax.broadcasted_iota(jnp.int32, s.shape, 1)
            return jnp.where(c <= r, s, -jnp.inf)

        for h in range(N_DIFF_HEADS):
            sl = slice(LANES * h, LANES * (h + 1))
            s = causal(_nt_dot(qd_s[h], dk_ref[0, pl.ds(k0, tq), sl]))
            _online_update(s, md_s, ld_s, accd_s, dv_ref[0, pl.ds(k0, tq), sl], h)
        kc = kcat_ref[0, pl.ds(k0, tq), :]
        s = causal(_nt_dot(qcat_ref[0].reshape(N_MLA_HEADS * tq, QCAT), kc))
        _online_update(s, mm_s, lm_s, accm_s, kc[:, :MLA_KV_RANK], slice(None))

    def body(kt, carry):
        key_tile(kt, False)
        return carry

    lax.fori_loop(0, qi, body, 0)
    key_tile(qi, True)

    lam = lam_ref[...]
    for h in range(N_DIFF_HEADS):
        o = accd_s[h] / jnp.sum(ld_s[h], axis=-1, keepdims=True)
        od_ref[0, :, LANES * h:LANES * (h + 1)] = o[:tq] - lam * o[tq:]
    om = accm_s[...] / jnp.sum(lm_s[...], axis=-1, keepdims=True)
    for h in range(N_MLA_HEADS):
        om_ref[0, :, MLA_KV_RANK * h:MLA_KV_RANK * (h + 1)] = om[h * tq:(h + 1) * tq].astype(BF16)


def _prompt_attn(lam, dq, qcat, dk, dv, kcat, *, tq):
    b, t, _ = dq.shape
    kv = lambda w: pl.BlockSpec((1, t, w), lambda bi, qi: (bi, 0, 0))
    return pl.pallas_call(
        functools.partial(_prompt_attn_kernel, tq=tq),
        grid=(b, t // tq),
        in_specs=[pl.BlockSpec((1, LANES), lambda bi, qi: (0, 0)),
                  pl.BlockSpec((1, tq, DQ_COLS), lambda bi, qi: (bi, qi, 0)),
                  pl.BlockSpec((1, N_MLA_HEADS, tq, QCAT), lambda bi, qi: (bi, 0, qi, 0)),
                  kv(DQ_COLS), kv(DQ_COLS), kv(QCAT)],
        out_specs=[pl.BlockSpec((1, tq, DIFF_WIDTH), lambda bi, qi: (bi, qi, 0)),
                   pl.BlockSpec((1, tq, N_MLA_HEADS * MLA_KV_RANK), lambda bi, qi: (bi, qi, 0))],
        out_shape=[jax.ShapeDtypeStruct((b, t, DIFF_WIDTH), F32),
                   jax.ShapeDtypeStruct((b, t, N_MLA_HEADS * MLA_KV_RANK), BF16)],
        scratch_shapes=[pltpu.VMEM((N_DIFF_HEADS, 2 * tq, LANES), BF16),
                        pltpu.VMEM((N_DIFF_HEADS, 2 * tq, LANES), F32),
                        pltpu.VMEM((N_DIFF_HEADS, 2 * tq, LANES), F32),
                        pltpu.VMEM((N_DIFF_HEADS, 2 * tq, DIFF_VDIM), F32),
                        pltpu.VMEM((N_MLA_HEADS * tq, LANES), F32),
                        pltpu.VMEM((N_MLA_HEADS * tq, LANES), F32),
                        pltpu.VMEM((N_MLA_HEADS * tq, MLA_KV_RANK), F32)],
        compiler_params=pltpu.CompilerParams(dimension_semantics=("parallel", "arbitrary"),
                                             vmem_limit_bytes=VMEM_LIMIT),
        name="prompt_attn",
    )(lam, dq, qcat, dk, dv, kcat)


def _decode_kernel(pt_ref, lam_ref, qd_ref, qm_ref, ks_ref, vs_ref, cs_ref,
                   kt_hbm, v_hbm, c_hbm, pe_hbm,
                   x_ref, sh_ref, sc_ref, g_ref, fn_ref, wgu_ref, wd_ref, final_ref,
                   od_ref, om_ref, y_ref,
                   kbuf, vbuf, cbuf, pbuf, sems, kpad, vpad, cpad, *, n_pages):
    ch = PAGES_PER_CHUNK
    n_chunks = n_pages // ch
    b = pl.program_id(0)
    nb = pl.num_programs(0)
    tok = ch * PAGE_SIZE
    n_rows = qd_ref.shape[1]
    heads_rows = n_rows // N_DIFF_HEADS

    def copies(bb, c, slot):
        out = []
        for j in range(ch):
            pg = pt_ref[bb, c * ch + j]
            out.append(pltpu.make_async_copy(kt_hbm.at[pg], kbuf.at[slot, :, pl.ds(j * PAGE_SIZE, PAGE_SIZE)],
                                             sems.at[0, slot]))
            out.append(pltpu.make_async_copy(v_hbm.at[pg], vbuf.at[slot, pl.ds(j * PAGE_SIZE * N_DIFF_HEADS,
                                                                              PAGE_SIZE * N_DIFF_HEADS), :],
                                             sems.at[1, slot]))
            out.append(pltpu.make_async_copy(c_hbm.at[pg], cbuf.at[slot, pl.ds(j * PAGE_SIZE, PAGE_SIZE), :],
                                             sems.at[2, slot]))
            out.append(pltpu.make_async_copy(pe_hbm.at[pg],
                                             pbuf.at[slot, pl.ds(0, MLA_ROPE), pl.ds(j * PAGE_SIZE, PAGE_SIZE)],
                                             sems.at[3, slot]))
        return out

    def start(cps):
        for i, cp in enumerate(cps):
            cp.start(priority=i % 2)

    @pl.when(b == 0)
    def _():
        pbuf[...] = jnp.zeros(pbuf.shape, F32)
        kpad[...] = jnp.zeros(kpad.shape, F32)
        vpad[...] = jnp.zeros(vpad.shape, F32)
        cpad[...] = jnp.zeros(cpad.shape, F32)
        for d in range(DECODE_SLOTS - 1):
            start(copies(d // n_chunks, d % n_chunks, d))

    y_ref[0] = _ffn_tile(x_ref[0], sh_ref[0], sc_ref[0], g_ref[0], fn_ref[...], wgu_ref, wd_ref, final_ref[...])

    qd = qd_ref[0]
    qm = qm_ref[0]
    qm_lat = qm[:, :MLA_KV_RANK]
    qm_pe = qm[:, MLA_KV_RANK:]

    def softmax_step(s, m, l):
        m_new = jnp.maximum(m, jnp.max(s, axis=-1, keepdims=True))
        alpha = jnp.exp2(m - m_new)
        p = jnp.exp2(s - m_new)
        return p.astype(BF16), alpha, m_new, alpha * l + jnp.sum(p, axis=-1, keepdims=True)

    def chunk_body(c, carry):
        md, ld, accd, mm, lm, accm = carry
        g = b * n_chunks + c
        slot = lax.rem(g, DECODE_SLOTS)
        ahead = g + (DECODE_SLOTS - 1)

        @pl.when(ahead < nb * n_chunks)
        def _():
            start(copies(lax.div(ahead, n_chunks), lax.rem(ahead, n_chunks), lax.rem(ahead, DECODE_SLOTS)))

        for cp in copies(b, c, slot):
            cp.wait()

        s_d = _dot(qd, kbuf[slot].astype(BF16))
        cb = cbuf[slot].astype(BF16)
        s_m = _nt_dot(qm_lat, cb) + _dot(qm_pe, pbuf[slot].astype(BF16))
        p_d, a_d, md, ld = softmax_step(s_d, md, ld)
        p_m, a_m, mm, lm = softmax_step(s_m, mm, lm)
        new_accd = []
        for h in range(N_DIFF_HEADS):
            rows = slice(heads_rows * h, heads_rows * (h + 1))
            vh = vbuf[slot, pl.ds(h, tok, stride=N_DIFF_HEADS), :].astype(BF16)
            new_accd.append(a_d[rows] * accd[h] + _dot(p_d, vh)[rows])
        accm = a_m * accm + _dot(p_m, cb)
        return md, ld, tuple(new_accd), mm, lm, accm

    init = (jnp.full((n_rows, 1), -jnp.inf, F32), jnp.zeros((n_rows, 1), F32),
            tuple(jnp.zeros((heads_rows, DIFF_VDIM), F32) for _ in range(N_DIFF_HEADS)),
            jnp.full((n_rows, 1), -jnp.inf, F32), jnp.zeros((n_rows, 1), F32),
            jnp.zeros((n_rows, MLA_KV_RANK), F32))
    md, ld, accd, mm, lm, accm = lax.fori_loop(0, n_chunks, chunk_body, init)

    n_new = ks_ref.shape[1]
    kpad[0:n_new, :] = ks_ref[0]
    vpad[0:n_new, :] = vs_ref[0]
    cpad[0:n_new, :] = cs_ref[0].astype(F32)
    qpos = lax.broadcasted_iota(jnp.int32, (n_rows, PAGE_SIZE), 0) % n_new
    kpos = lax.broadcasted_iota(jnp.int32, (n_rows, PAGE_SIZE), 1)
    ok = kpos <= qpos
    cself = cpad[...].astype(BF16)
    s_d = jnp.where(ok, _nt_dot(qd, kpad[...].astype(BF16)), -jnp.inf)
    s_m = jnp.where(ok, _nt_dot(qm, cself), -jnp.inf)
    p_d, a_d, md, ld = softmax_step(s_d, md, ld)
    p_m, a_m, mm, lm = softmax_step(s_m, mm, lm)
    pv = _dot(p_d, vpad[...].astype(BF16))
    accm = a_m * accm + _dot(p_m, cself[:, :MLA_KV_RANK])

    lam = lam_ref[...]
    half = heads_rows // 2
    for h in range(N_DIFF_HEADS):
        rows = slice(heads_rows * h, heads_rows * (h + 1))
        acc = a_d[rows] * accd[h] + pv[rows, DIFF_VDIM * h:DIFF_VDIM * (h + 1)]
        o = acc / ld[rows]
        od_ref[0, h] = o[:half] - lam * o[half:]
    om_ref[0] = accm / lm


def _decode_attn(page_table, lam, qd_blk, qm, k_self, v_self, kc_self, kt_pages, v_pages, c_pages, pe_pages,
                 x_ffn, mod_ffn, ffn_norm, w_gu, w_down, final_norm):
    nb, n_pages = page_table.shape
    assert x_ffn.shape[0] == nb and nb % mod_ffn.shape[0] == 0
    tiles_per_group = nb // mod_ffn.shape[0]
    ffn_tile = pl.BlockSpec((1, x_ffn.shape[1], D_MODEL), lambda b, pt: (b, 0, 0))
    ffn_mod = lambda col: pl.BlockSpec((1, 1, D_MODEL), lambda b, pt: (b // tiles_per_group, 0, col))
    n_rows = qd_blk.shape[1]
    n_new = k_self.shape[1]
    ch = PAGES_PER_CHUNK
    tok = ch * PAGE_SIZE
    row_blk = lambda w: pl.BlockSpec((1, n_rows, w), lambda b, pt: (b, 0, 0))
    new_blk = lambda w: pl.BlockSpec((1, n_new, w), lambda b, pt: (b, 0, 0))
    any_spec = pl.BlockSpec(memory_space=pl.ANY)
    grid_spec = pltpu.PrefetchScalarGridSpec(
        num_scalar_prefetch=1,
        grid=(nb,),
        in_specs=[pl.BlockSpec((1, LANES), lambda b, pt: (0, 0)),
                  row_blk(DQ_COLS), row_blk(QCAT), new_blk(DQ_COLS), new_blk(DQ_COLS), new_blk(QCAT),
                  any_spec, any_spec, any_spec, any_spec,
                  ffn_tile, ffn_mod(3), ffn_mod(4), ffn_mod(5),
                  _const_spec((1, D_MODEL)), _const_spec((D_MODEL, 2 * D_FF)), _const_spec((D_FF, D_MODEL)),
                  _const_spec((1, D_MODEL))],
        out_specs=[pl.BlockSpec((1, N_DIFF_HEADS, n_new, DIFF_VDIM), lambda b, pt: (b, 0, 0, 0)),
                   pl.BlockSpec((1, n_rows, MLA_KV_RANK), lambda b, pt: (b, 0, 0)),
                   ffn_tile],
        scratch_shapes=[pltpu.VMEM((DECODE_SLOTS, DQ_COLS, tok), F32),
                        pltpu.VMEM((DECODE_SLOTS, tok * N_DIFF_HEADS, DIFF_VDIM), F32),
                        pltpu.VMEM((DECODE_SLOTS, tok, MLA_KV_RANK), F32),
                        pltpu.VMEM((DECODE_SLOTS, LANES, tok), F32),
                        pltpu.SemaphoreType.DMA((4, DECODE_SLOTS)),
                        pltpu.VMEM((PAGE_SIZE, DQ_COLS), F32),
                        pltpu.VMEM((PAGE_SIZE, DQ_COLS), F32),
                        pltpu.VMEM((PAGE_SIZE, QCAT), F32)])
    return pl.pallas_call(
        functools.partial(_decode_kernel, n_pages=n_pages),
        grid_spec=grid_spec,
        out_shape=[jax.ShapeDtypeStruct((nb, N_DIFF_HEADS, n_new, DIFF_VDIM), F32),
                   jax.ShapeDtypeStruct((nb, n_rows, MLA_KV_RANK), F32),
                   jax.ShapeDtypeStruct(x_ffn.shape, F32)],
        compiler_params=pltpu.CompilerParams(dimension_semantics=("arbitrary",),
                                             vmem_limit_bytes=VMEM_LIMIT),
        name="decode_attn",
    )(page_table, lam, qd_blk, qm, k_self, v_self, kc_self, kt_pages, v_pages, c_pages, pe_pages,
      x_ffn, mod_ffn, mod_ffn, mod_ffn, ffn_norm, w_gu, w_down, final_norm)


def _out_kernel(x_ref, g_ref, od_ref, om_ref, sub_ref, wuv_ref, wo_ref, o_ref):
    parts = []
    for h in range(N_DIFF_HEADS):
        o = od_ref[0, :, DIFF_VDIM * h:DIFF_VDIM * (h + 1)]
        parts.append((_rms(o, sub_ref[...]) * (1.0 - LAMBDA_INIT)).astype(BF16))
    pair = 2 * MLA_KV_RANK
    for j in range(N_MLA_HEADS // 2):
        parts.append(_dot(om_ref[0, :, pair * j:pair * (j + 1)], wuv_ref[j]).astype(BF16))
    mixed = jnp.concatenate(parts, axis=1)
    o_ref[0] = x_ref[0] + g_ref[0] * _dot(mixed, wo_ref[...])


def _out_proj(x, mod, od, om, diff_subln, w_uv_pairs, w_o, *, tm):
    nb, t, _ = x.shape
    r = mod.shape[1]
    rb = 1 if r == 1 else tm
    gate_idx = (lambda b, i: (b, 0, 2)) if r == 1 else (lambda b, i: (b, i, 2))
    tok = lambda w: pl.BlockSpec((1, tm, w), lambda b, i: (b, i, 0))
    return pl.pallas_call(
        _out_kernel,
        grid=(nb, t // tm),
        in_specs=[tok(D_MODEL), pl.BlockSpec((1, rb, D_MODEL), gate_idx),
                  tok(DIFF_WIDTH), tok(N_MLA_HEADS * MLA_KV_RANK),
                  _const_spec((1, DIFF_VDIM)),
                  _const_spec((N_MLA_HEADS // 2, 2 * MLA_KV_RANK, LANES)),
                  _const_spec((DIFF_WIDTH + MLA_WIDTH, D_MODEL))],
        out_specs=tok(D_MODEL),
        out_shape=jax.ShapeDtypeStruct((nb, t, D_MODEL), F32),
        compiler_params=pltpu.CompilerParams(dimension_semantics=("parallel", "parallel"),
                                             vmem_limit_bytes=VMEM_LIMIT),
        name="out_proj",
    )(x, mod, od, om, diff_subln, w_uv_pairs, w_o)


FF_CHUNK = 256


def _ffn_tile(x, sh, sc, g, ffn_norm, wgu_ref, wd_ref, final_norm):
    h = (_rms(x, ffn_norm) * (1.0 + sc) + sh).astype(BF16)
    acc = jnp.zeros(x.shape, F32)
    for c in range(D_FF // FF_CHUNK):
        lo = FF_CHUNK * c
        gate = _dot(h, wgu_ref[:, lo:lo + FF_CHUNK])
        up = _dot(h, wgu_ref[:, D_FF + lo:D_FF + lo + FF_CHUNK])
        act = (gate * jax.nn.sigmoid(gate) * up).astype(BF16)
        acc = acc + _dot(act, wd_ref[lo:lo + FF_CHUNK, :])
    return _rms(x + g * acc, final_norm)


def _ffn_kernel(x_ref, sh_ref, sc_ref, g_ref, fn_ref, wgu_ref, wd_ref, final_ref, y_ref):
    y_ref[0] = _ffn_tile(x_ref[0], sh_ref[0], sc_ref[0], g_ref[0], fn_ref[...], wgu_ref, wd_ref, final_ref[...])


def _ffn(x, mod, ffn_norm, w_gu, w_down, final_norm, *, tm):
    nb, t, _ = x.shape
    r = mod.shape[1]
    rb = 1 if r == 1 else tm
    mod_idx = (lambda col: (lambda b, i: (b, 0, col))) if r == 1 else (lambda col: (lambda b, i: (b, i, col)))
    tok = pl.BlockSpec((1, tm, D_MODEL), lambda b, i: (b, i, 0))
    return pl.pallas_call(
        _ffn_kernel,
        grid=(nb, t // tm),
        in_specs=[tok,
                  pl.BlockSpec((1, rb, D_MODEL), mod_idx(3)),
                  pl.BlockSpec((1, rb, D_MODEL), mod_idx(4)),
                  pl.BlockSpec((1, rb, D_MODEL), mod_idx(5)),
                  _const_spec((1, D_MODEL)),
                  _const_spec((D_MODEL, 2 * D_FF)),
                  _const_spec((D_FF, D_MODEL)),
                  _const_spec((1, D_MODEL))],
        out_specs=tok,
        out_shape=jax.ShapeDtypeStruct((nb, t, D_MODEL), F32),
        compiler_params=pltpu.CompilerParams(dimension_semantics=("parallel", "parallel"),
                                             vmem_limit_bytes=VMEM_LIMIT),
        name="ffn",
    )(x, mod, mod, mod, ffn_norm, w_gu, w_down, final_norm)


def _rope_tables(pos, rot, period):
    half = rot // 2
    j = np.arange(LANES) % period
    first = j < half
    second = (j >= half) & (j < rot)
    freq_idx = np.where(first, j, np.clip(j - half, 0, half - 1)).astype(np.float32)
    inv = ROPE_THETA ** (-(jnp.asarray(freq_idx) * 2.0 / rot))
    ang = pos.astype(F32)[:, None] * inv[None, :]
    cos, sin = jnp.cos(ang), jnp.sin(ang)
    return jnp.stack([jnp.where(first | second, cos, 1.0),
                      jnp.where(first, -sin, 0.0),
                      jnp.where(second, sin, 0.0)])


def _prep_weights(w_in, w_uq, w_ukv, w_o, w_gate_up, w_down):
    w_in_p = jnp.pad(w_in, ((0, 0), (0, IN_COLS_PAD - IN_COLS))).astype(BF16)
    uq = w_uq.reshape(MLA_Q_RANK, N_MLA_HEADS, MLA_NOPE + MLA_ROPE)
    w_uq_p = jnp.concatenate([uq[:, :, :MLA_NOPE].reshape(MLA_Q_RANK, -1),
                              uq[:, :, MLA_NOPE:].reshape(MLA_Q_RANK, -1)], axis=1).astype(BF16)
    ukv = w_ukv.reshape(MLA_KV_RANK, N_MLA_HEADS, MLA_NOPE + MLA_V)
    uk_t = jnp.transpose(ukv[:, :, :MLA_NOPE], (1, 2, 0))
    uv = jnp.transpose(ukv[:, :, MLA_NOPE:], (1, 0, 2))
    zk = jnp.zeros_like(uk_t[0])
    zv = jnp.zeros_like(uv[0])
    uk_pairs = jnp.stack([jnp.block([[uk_t[2 * j], zk], [zk, uk_t[2 * j + 1]]])
                          for j in range(N_MLA_HEADS // 2)]).astype(BF16)
    uv_pairs = jnp.stack([jnp.block([[uv[2 * j], zv], [zv, uv[2 * j + 1]]])
                          for j in range(N_MLA_HEADS // 2)]).astype(BF16)
    return w_in_p, w_uq_p, uk_pairs, uv_pairs, w_o.astype(BF16), w_gate_up.astype(BF16), w_down.astype(BF16)


def kernel(x_prompt, x_sample, c_prompt, c_sample, cache_k_diff, cache_v_diff, cache_ckv, cache_kpe, page_table, mod_w, mod_b, attn_norm, w_in, q_norm, kv_norm, w_uq, w_ukv, lambda_q1, lambda_k1, lambda_q2, lambda_k2, diff_subln, w_o, ffn_norm, w_gate_up, w_down, final_norm):
    assert mod_w.shape[0] == 1, "single-layer trunk"
    bp, tp, _ = x_prompt.shape
    bs, ts, _ = x_sample.shape
    n_pool = cache_ckv.shape[1]
    n_pages = page_table.shape[1]
    assert n_pages % PAGES_PER_CHUNK == 0 and bs * (n_pages // PAGES_PER_CHUNK) >= DECODE_SLOTS - 1
    ns = bs * ts

    w_in_p, w_uq_p, uk_pairs, uv_pairs, w_o_b, w_gu_b, w_down_b = _prep_weights(
        w_in[0], w_uq[0], w_ukv[0], w_o[0], w_gate_up[0], w_down[0])
    final_norm2 = final_norm.reshape(1, D_MODEL)

    lam_vecs = jnp.concatenate([lambda_q1, lambda_k1, lambda_q2, lambda_k2], axis=0)
    mod_p, mod_s, lam = _modulation(c_prompt, jnp.repeat(c_sample, ts, axis=0), mod_w[0], mod_b, lam_vecs)
    mod_p = mod_p.reshape(bp, 1, 6 * D_MODEL)
    mod_s = mod_s.reshape(1, ns, 6 * D_MODEL)

    pos_p = jnp.arange(tp)
    pos_s = n_pages * PAGE_SIZE + (jnp.arange(ns) % ts)
    shared = (attn_norm, w_in_p, q_norm, kv_norm, w_uq_p, uk_pairs)

    tm_p = min(256, tp)
    dq, dk_f, dk_b, dv_f, dv_b, ckv_f, kpe_f, kcat, qcat = _in_proj(
        x_prompt, mod_p, _rope_tables(pos_p, DIFF_ROT, DIFF_DH), _rope_tables(pos_p, MLA_ROPE, MLA_ROPE),
        *shared, tm=tm_p)
    od_p, om_p = _prompt_attn(lam, dq, qcat, dk_b, dv_b, kcat, tq=tm_p)
    tm_f = min(512, tp)
    x1_p = _out_proj(x_prompt, mod_p, od_p, om_p, diff_subln, uv_pairs, w_o_b, tm=tm_f)
    ffn_rows = bp * tp // bs
    assert bp * tp % bs == 0 and tp % ffn_rows == 0 and ffn_rows % 8 == 0

    xs = x_sample.reshape(1, ns, D_MODEL)
    tm_s = min(256, ns)
    sq, sk_f, _, sv_f, _, sckv_f, skpe_f, skcat, sqcat = _in_proj(
        xs, mod_s, _rope_tables(pos_s, DIFF_ROT, DIFF_DH), _rope_tables(pos_s, MLA_ROPE, MLA_ROPE),
        *shared, tm=tm_s)
    n_hm = 2 * N_DIFF_HEADS
    sq5 = jnp.transpose(sq.reshape(bs, ts, n_hm, DIFF_DH), (0, 2, 1, 3))
    qd_blk = (sq5[:, :, :, None, :] * jnp.eye(n_hm, dtype=BF16)[None, :, None, :, None]
              ).reshape(bs, n_hm * ts, DQ_COLS)
    qm = jnp.transpose(sqcat.reshape(N_MLA_HEADS, bs, ts, QCAT), (1, 0, 2, 3)).reshape(bs, N_MLA_HEADS * ts, QCAT)
    kt_pages = jnp.transpose(cache_k_diff[0], (0, 2, 3, 4, 1)).reshape(n_pool, DQ_COLS, PAGE_SIZE)
    v_pages = cache_v_diff[0].reshape(n_pool, PAGE_SIZE * N_DIFF_HEADS, DIFF_VDIM)
    pe_pages = jnp.transpose(cache_kpe[0], (0, 2, 1))
    od_s, om_s, y_prompt = _decode_attn(page_table, lam, qd_blk, qm,
                                        sk_f.reshape(bs, ts, DQ_COLS), sv_f.reshape(bs, ts, DQ_COLS),
                                        skcat.reshape(bs, ts, QCAT), kt_pages, v_pages, cache_ckv[0], pe_pages,
                                        x1_p.reshape(bs, ffn_rows, D_MODEL), mod_p,
                                        ffn_norm, w_gu_b, w_down_b, final_norm2)
    y_prompt = y_prompt.reshape(bp, tp, D_MODEL)
    od_s = jnp.transpose(od_s, (0, 2, 1, 3)).reshape(1, ns, DIFF_WIDTH)
    om_s = jnp.transpose(om_s.reshape(bs, N_MLA_HEADS, ts, MLA_KV_RANK), (0, 2, 1, 3)
                         ).reshape(1, ns, N_MLA_HEADS * MLA_KV_RANK).astype(BF16)
    tm_fs = min(512, ns)
    x1_s = _out_proj(xs, mod_s, od_s, om_s, diff_subln, uv_pairs, w_o_b, tm=tm_fs)
    y_sample = _ffn(x1_s, mod_s, ffn_norm, w_gu_b, w_down_b, final_norm2, tm=tm_fs).reshape(bs, ts, D_MODEL)

    return (y_prompt, y_sample,
            dk_f.reshape(1, bp, tp, N_DIFF_HEADS, 2, DIFF_DH),
            dv_f.reshape(1, bp, tp, N_DIFF_HEADS, DIFF_VDIM),
            ckv_f.reshape(1, bp, tp, MLA_KV_RANK),
            kpe_f.reshape(1, bp, tp, MLA_ROPE),
            sk_f.reshape(1, bs, ts, N_DIFF_HEADS, 2, DIFF_DH),
            sv_f.reshape(1, bs, ts, N_DIFF_HEADS, DIFF_VDIM),
            sckv_f.reshape(1, bs, ts, MLA_KV_RANK),
            skpe_f.reshape(1, bs, ts, MLA_ROPE))
```

```python
import functools

import jax
import jax.numpy as jnp
import numpy as np
from jax import lax
from jax.experimental import pallas as pl
from jax.experimental.pallas import tpu as pltpu

F32 = jnp.float32
BF16 = jnp.bfloat16

D_MODEL = 1024
N_DIFF_HEADS = 4
DIFF_DH = 64
DIFF_VDIM = 2 * DIFF_DH
DIFF_ROT = DIFF_DH // 4
N_MLA_HEADS = 8
MLA_NOPE = 64
MLA_ROPE = 32
MLA_V = 64
MLA_Q_RANK = 384
MLA_KV_RANK = 256
DIFF_WIDTH = N_DIFF_HEADS * DIFF_VDIM
MLA_WIDTH = N_MLA_HEADS * MLA_V
DQ_COLS = N_DIFF_HEADS * 2 * DIFF_DH
IN_COLS = 3 * DQ_COLS + MLA_Q_RANK + MLA_KV_RANK + MLA_ROPE
D_FF = 2816
ROPE_THETA = 500000.0
NORM_EPS = 1e-6
PAGE_SIZE = 128
LOG2E = 1.4426950408889634
DIFF_SCALE = DIFF_DH ** -0.5 * LOG2E
MLA_SCALE = (MLA_NOPE + MLA_ROPE) ** -0.5 * LOG2E
LAMBDA_INIT = 0.8 - 0.6

LANES = 128
IN_COLS_PAD = 2304
QCAT = MLA_KV_RANK + LANES
PAGES_PER_CHUNK = 8
DECODE_SLOTS = 4
FFN_SPLIT = 4
VMEM_LIMIT = 56 * 1024 * 1024


def _nt_dot(a, b):
    return lax.dot_general(a, b, (((1,), (1,)), ((), ())), preferred_element_type=F32)


def _dot(a, b):
    return jnp.dot(a, b, preferred_element_type=F32)


def _rms(x, g):
    return x * lax.rsqrt(jnp.mean(x * x, axis=-1, keepdims=True) + NORM_EPS) * g


def _rope(x, tab_ref, half):
    return (x * tab_ref[0]
            + pltpu.roll(x, LANES - half, 1) * tab_ref[1]
            + pltpu.roll(x, half, 1) * tab_ref[2])


def _const_spec(shape):
    nd = len(shape)
    return pl.BlockSpec(shape, lambda *_: (0,) * nd, pipeline_mode=pl.Buffered(1))


def _mod_kernel(cp_ref, cs_ref, w_ref, b_ref, lv_ref, mp_ref, ms_ref, lam_ref):
    w = w_ref[...].astype(BF16)
    for c_ref, m_ref in ((cp_ref, mp_ref), (cs_ref, ms_ref)):
        c = c_ref[...]
        m_ref[...] = _dot((c * jax.nn.sigmoid(c)).astype(BF16), w) + b_ref[...]
    lv = lv_ref[...]
    s1 = jnp.sum(lv[0:1] * lv[1:2], axis=-1, keepdims=True)
    s2 = jnp.sum(lv[2:3] * lv[3:4], axis=-1, keepdims=True)
    lam_ref[...] = jnp.broadcast_to(jnp.exp(s1) - jnp.exp(s2) + LAMBDA_INIT, lam_ref.shape)


def _modulation(c_p, c_s, mod_w, mod_b, lam_vecs):
    n_p, n_s = c_p.shape[0], c_s.shape[0]
    tn = 1024
    return pl.pallas_call(
        _mod_kernel,
        grid=(6 * D_MODEL // tn,),
        in_specs=[pl.BlockSpec((n_p, D_MODEL), lambda j: (0, 0)),
                  pl.BlockSpec((n_s, D_MODEL), lambda j: (0, 0)),
                  pl.BlockSpec((D_MODEL, tn), lambda j: (0, j)),
                  pl.BlockSpec((1, tn), lambda j: (0, j)),
                  pl.BlockSpec((4, DIFF_DH), lambda j: (0, 0))],
        out_specs=[pl.BlockSpec((n_p, tn), lambda j: (0, j)),
                   pl.BlockSpec((n_s, tn), lambda j: (0, j)),
                   pl.BlockSpec((1, LANES), lambda j: (0, 0))],
        out_shape=[jax.ShapeDtypeStruct((n_p, 6 * D_MODEL), F32),
                   jax.ShapeDtypeStruct((n_s, 6 * D_MODEL), F32),
                   jax.ShapeDtypeStruct((1, LANES), F32)],
        compiler_params=pltpu.CompilerParams(dimension_semantics=("arbitrary",)),
        name="modulation",
    )(c_p, c_s, mod_w, mod_b, lam_vecs)


def _in_kernel(x_ref, sh_ref, sc_ref, rd_ref, rp_ref, an_ref, win_ref, qn_ref, kn_ref, wuq_ref, wuk_ref,
               dq_ref, dkf_ref, dkb_ref, dvf_ref, dvb_ref, ckvf_ref, kpef_ref, kcat_ref, qcat_ref):
    x = x_ref[0]
    tm = x.shape[0]
    h = (_rms(x, an_ref[...]) * (1.0 + sc_ref[0]) + sh_ref[0]).astype(BF16)
    z = _dot(h, win_ref[...])
    for j in range(DQ_COLS // LANES):
        sl = slice(LANES * j, LANES * (j + 1))
        dq_ref[0, :, sl] = (_rope(z[:, sl], rd_ref, DIFF_ROT // 2) * DIFF_SCALE).astype(BF16)
        k = _rope(z[:, DQ_COLS + LANES * j:DQ_COLS + LANES * (j + 1)], rd_ref, DIFF_ROT // 2)
        dkf_ref[0, :, sl] = k
        dkb_ref[0, :, sl] = k.astype(BF16)
    dv = z[:, 2 * DQ_COLS:3 * DQ_COLS]
    for hh in range(N_DIFF_HEADS):
        dvf_ref[0, pl.ds(hh, tm, stride=N_DIFF_HEADS), :] = dv[:, DIFF_VDIM * hh:DIFF_VDIM * (hh + 1)]
    dvb_ref[0] = dv.astype(BF16)

    c0 = 3 * DQ_COLS
    cqn = _rms(z[:, c0:c0 + MLA_Q_RANK], qn_ref[...]).astype(BF16)
    q = _dot(cqn, wuq_ref[...])
    nope_w = N_MLA_HEADS * MLA_NOPE
    qn = q[:, :nope_w].astype(BF16)
    for j in range(N_MLA_HEADS // 2):
        ql = _dot(qn[:, LANES * j:LANES * (j + 1)], wuk_ref[j])
        qcat_ref[0, 2 * j, :, 0:MLA_KV_RANK] = (ql[:, :MLA_KV_RANK] * MLA_SCALE).astype(BF16)
        qcat_ref[0, 2 * j + 1, :, 0:MLA_KV_RANK] = (ql[:, MLA_KV_RANK:] * MLA_SCALE).astype(BF16)
    lane = lax.broadcasted_iota(jnp.int32, (tm, LANES), 1)
    heads_per_slab = LANES // MLA_ROPE
    for g in range(N_MLA_HEADS // heads_per_slab):
        qp = _rope(q[:, nope_w + LANES * g:nope_w + LANES * (g + 1)], rp_ref, MLA_ROPE // 2) * MLA_SCALE
        for i in range(heads_per_slab):
            moved = qp if i == 0 else pltpu.roll(qp, LANES - MLA_ROPE * i, 1)
            qcat_ref[0, heads_per_slab * g + i, :, MLA_KV_RANK:QCAT] = (
                jnp.where(lane < MLA_ROPE, moved, 0.0).astype(BF16))

    c1 = c0 + MLA_Q_RANK
    ckvn = _rms(z[:, c1:c1 + MLA_KV_RANK], kn_ref[...])
    ckvf_ref[0] = ckvn
    kpe = _rope(z[:, c1 + MLA_KV_RANK:IN_COLS_PAD], rp_ref, MLA_ROPE // 2)
    kpef_ref[0] = kpe[:, :MLA_ROPE]
    kcat_ref[0, :, 0:MLA_KV_RANK] = ckvn.astype(BF16)
    kcat_ref[0, :, MLA_KV_RANK:QCAT] = kpe.astype(BF16)


def _in_proj(x, mod, rope_d, rope_p, attn_norm, w_in_p, q_norm, kv_norm, w_uq_p, w_uk_pairs, *, tm):
    nb, t, _ = x.shape
    r = mod.shape[1]
    rb = 1 if r == 1 else tm
    mod_idx = (lambda col: (lambda b, i: (b, 0, col))) if r == 1 else (lambda col: (lambda b, i: (b, i, col)))
    tok = lambda w: pl.BlockSpec((1, tm, w), lambda b, i: (b, i, 0))
    rope_spec = pl.BlockSpec((3, tm, LANES), lambda b, i: (0, i, 0))
    out_shape = [jax.ShapeDtypeStruct((nb, t, DQ_COLS), BF16),
                 jax.ShapeDtypeStruct((nb, t, DQ_COLS), F32), jax.ShapeDtypeStruct((nb, t, DQ_COLS), BF16),
                 jax.ShapeDtypeStruct((nb, t * N_DIFF_HEADS, DIFF_VDIM), F32),
                 jax.ShapeDtypeStruct((nb, t, DQ_COLS), BF16),
                 jax.ShapeDtypeStruct((nb, t, MLA_KV_RANK), F32),
                 jax.ShapeDtypeStruct((nb, t, MLA_ROPE), F32),
                 jax.ShapeDtypeStruct((nb, t, QCAT), BF16),
                 jax.ShapeDtypeStruct((nb, N_MLA_HEADS, t, QCAT), BF16)]
    out_specs = [tok(DQ_COLS), tok(DQ_COLS), tok(DQ_COLS),
                 pl.BlockSpec((1, tm * N_DIFF_HEADS, DIFF_VDIM), lambda b, i: (b, i, 0)), tok(DQ_COLS),
                 tok(MLA_KV_RANK), tok(MLA_ROPE), tok(QCAT),
                 pl.BlockSpec((1, N_MLA_HEADS, tm, QCAT), lambda b, i: (b, 0, i, 0))]
    return pl.pallas_call(
        _in_kernel,
        grid=(nb, t // tm),
        in_specs=[tok(D_MODEL),
                  pl.BlockSpec((1, rb, D_MODEL), mod_idx(0)),
                  pl.BlockSpec((1, rb, D_MODEL), mod_idx(1)),
                  rope_spec, rope_spec,
                  _const_spec((1, D_MODEL)),
                  _const_spec((D_MODEL, IN_COLS_PAD)),
                  _const_spec((1, MLA_Q_RANK)),
                  _const_spec((1, MLA_KV_RANK)),
                  _const_spec((MLA_Q_RANK, N_MLA_HEADS * (MLA_NOPE + MLA_ROPE))),
                  _const_spec((N_MLA_HEADS // 2, LANES, 2 * MLA_KV_RANK))],
        out_specs=out_specs,
        out_shape=out_shape,
        compiler_params=pltpu.CompilerParams(dimension_semantics=("parallel", "parallel"),
                                             vmem_limit_bytes=VMEM_LIMIT),
        name="in_proj",
    )(x, mod, mod, rope_d, rope_p, attn_norm, w_in_p, q_norm, kv_norm, w_uq_p, w_uk_pairs)


def _online_update(s, m_ref, l_ref, acc_ref, v, idx):
    blocks = [s[:, LANES * j:LANES * (j + 1)] for j in range(s.shape[1] // LANES)]
    m_old = m_ref[idx]
    m_new = jnp.maximum(m_old, jnp.max(functools.reduce(jnp.maximum, blocks), axis=-1, keepdims=True))
    alpha = jnp.exp2(m_old - m_new)
    p = [jnp.exp2(blk - m_new) for blk in blocks]
    l_ref[idx] = alpha * l_ref[idx] + functools.reduce(jnp.add, p)
    pv = _dot(jnp.concatenate(p, axis=1).astype(BF16), v)
    acc = acc_ref[idx]
    scale = jnp.concatenate([alpha] * (acc.shape[1] // LANES), axis=1)
    acc_ref[idx] = scale * acc + pv
    m_ref[idx] = m_new


def _prompt_attn_kernel(lam_ref, dq_ref, qcat_ref, dk_ref, dv_ref, kcat_ref, od_ref, om_ref,
                        qd_s, md_s, ld_s, accd_s, mm_s, lm_s, accm_s, *, tq):
    qi = pl.program_id(1)
    lane = lax.broadcasted_iota(jnp.int32, (tq, LANES), 1)
    for h in range(N_DIFF_HEADS):
        qh = dq_ref[0, :, LANES * h:LANES * (h + 1)]
        qd_s[h, 0:tq, :] = jnp.where(lane < DIFF_DH, qh, jnp.zeros_like(qh))
        qd_s[h, tq:2 * tq, :] = jnp.where(lane >= DIFF_DH, qh, jnp.zeros_like(qh))
    md_s[...] = jnp.full(md_s.shape, -jnp.inf, F32)
    ld_s[...] = jnp.zeros(ld_s.shape, F32)
    accd_s[...] = jnp.zeros(accd_s.shape, F32)
    mm_s[...] = jnp.full(mm_s.shape, -jnp.inf, F32)
    lm_s[...] = jnp.zeros(lm_s.shape, F32)
    accm_s[...] = jnp.zeros(accm_s.shape, F32)

    def key_tile(kt, diagonal):
        k0 = pl.multiple_of(kt * tq, tq)

        def causal(s):
            if not diagonal:
                return s
            r = lax.broadcasted_iota(jnp.int32, s.shape, 0) % tq
            c = lax.broadcasted_iota(jnp.int32, s.shape, 1)
            return jnp.where(c <= r, s, -jnp.inf)

        for h in range(N_DIFF_HEADS):
            sl = slice(LANES * h, LANES * (h + 1))
            s = causal(_nt_dot(qd_s[h], dk_ref[0, pl.ds(k0, tq), sl]))
            _online_update(s, md_s, ld_s, accd_s, dv_ref[0, pl.ds(k0, tq), sl], h)
        kc = kcat_ref[0, pl.ds(k0, tq), :]
        s = causal(_nt_dot(qcat_ref[0].reshape(N_MLA_HEADS * tq, QCAT), kc))
        _online_update(s, mm_s, lm_s, accm_s, kc[:, :MLA_KV_RANK], slice(None))

    def body(kt, carry):
        key_tile(kt, False)
        return carry

    lax.fori_loop(0, qi, body, 0)
    key_tile(qi, True)

    lam = lam_ref[...]
    for h in range(N_DIFF_HEADS):
        o = accd_s[h] / jnp.sum(ld_s[h], axis=-1, keepdims=True)
        od_ref[0, :, LANES * h:LANES * (h + 1)] = o[:tq] - lam * o[tq:]
    om = accm_s[...] / jnp.sum(lm_s[...], axis=-1, keepdims=True)
    for h in range(N_MLA_HEADS):
        om_ref[0, :, MLA_KV_RANK * h:MLA_KV_RANK * (h + 1)] = om[h * tq:(h + 1) * tq].astype(BF16)


def _prompt_attn(lam, dq, qcat, dk, dv, kcat, *, tq):
    b, t, _ = dq.shape
    kv = lambda w: pl.BlockSpec((1, t, w), lambda bi, qi: (bi, 0, 0))
    return pl.pallas_call(
        functools.partial(_prompt_attn_kernel, tq=tq),
        grid=(b, t // tq),
        in_specs=[pl.BlockSpec((1, LANES), lambda bi, qi: (0, 0)),
                  pl.BlockSpec((1, tq, DQ_COLS), lambda bi, qi: (bi, qi, 0)),
                  pl.BlockSpec((1, N_MLA_HEADS, tq, QCAT), lambda bi, qi: (bi, 0, qi, 0)),
                  kv(DQ_COLS), kv(DQ_COLS), kv(QCAT)],
        out_specs=[pl.BlockSpec((1, tq, DIFF_WIDTH), lambda bi, qi: (bi, qi, 0)),
                   pl.BlockSpec((1, tq, N_MLA_HEADS * MLA_KV_RANK), lambda bi, qi: (bi, qi, 0))],
        out_shape=[jax.ShapeDtypeStruct((b, t, DIFF_WIDTH), F32),
                   jax.ShapeDtypeStruct((b, t, N_MLA_HEADS * MLA_KV_RANK), BF16)],
        scratch_shapes=[pltpu.VMEM((N_DIFF_HEADS, 2 * tq, LANES), BF16),
                        pltpu.VMEM((N_DIFF_HEADS, 2 * tq, LANES), F32),
                        pltpu.VMEM((N_DIFF_HEADS, 2 * tq, LANES), F32),
                        pltpu.VMEM((N_DIFF_HEADS, 2 * tq, DIFF_VDIM), F32),
                        pltpu.VMEM((N_MLA_HEADS * tq, LANES), F32),
                        pltpu.VMEM((N_MLA_HEADS * tq, LANES), F32),
                        pltpu.VMEM((N_MLA_HEADS * tq, MLA_KV_RANK), F32)],
        compiler_params=pltpu.CompilerParams(dimension_semantics=("parallel", "arbitrary"),
                                             vmem_limit_bytes=VMEM_LIMIT),
        name="prompt_attn",
    )(lam, dq, qcat, dk, dv, kcat)


def _decode_kernel(pt_ref, lam_ref, qd_ref, qm_ref, ks_ref, vs_ref, cs_ref,
                   kt_hbm, v_hbm, c_hbm, pe_hbm,
                   x_ref, sh_ref, sc_ref, g_ref, fn_ref, wgu_ref, wd_ref, final_ref,
                   od_ref, om_ref, y_ref,
                   kbuf, vbuf, cbuf, pbuf, sems, kpad, vpad, cpad, hbuf, facc, *, n_pages):
    ch = PAGES_PER_CHUNK
    n_chunks = n_pages // ch
    b = pl.program_id(0)
    nb = pl.num_programs(0)
    tok = ch * PAGE_SIZE
    n_rows = qd_ref.shape[1]
    heads_rows = n_rows // N_DIFF_HEADS

    def copies(bb, c, slot):
        out = []
        for j in range(ch):
            pg = pt_ref[bb, c * ch + j]
            out.append(pltpu.make_async_copy(kt_hbm.at[pg], kbuf.at[slot, :, pl.ds(j * PAGE_SIZE, PAGE_SIZE)],
                                             sems.at[0, slot]))
            out.append(pltpu.make_async_copy(v_hbm.at[pg], vbuf.at[slot, pl.ds(j * PAGE_SIZE * N_DIFF_HEADS,
                                                                              PAGE_SIZE * N_DIFF_HEADS), :],
                                             sems.at[1, slot]))
            out.append(pltpu.make_async_copy(c_hbm.at[pg], cbuf.at[slot, pl.ds(j * PAGE_SIZE, PAGE_SIZE), :],
                                             sems.at[2, slot]))
            out.append(pltpu.make_async_copy(pe_hbm.at[pg],
                                             pbuf.at[slot, pl.ds(0, MLA_ROPE), pl.ds(j * PAGE_SIZE, PAGE_SIZE)],
                                             sems.at[3, slot]))
        return out

    def start(cps):
        for i, cp in enumerate(cps):
            cp.start(priority=i % 2)

    @pl.when(b == 0)
    def _():
        pbuf[...] = jnp.zeros(pbuf.shape, F32)
        kpad[...] = jnp.zeros(kpad.shape, F32)
        vpad[...] = jnp.zeros(vpad.shape, F32)
        cpad[...] = jnp.zeros(cpad.shape, F32)
        for d in range(DECODE_SLOTS - 1):
            start(copies(d // n_chunks, d % n_chunks, d))

    part = lax.rem(b, FFN_SPLIT)
    n_ff = D_FF // FF_CHUNK

    @pl.when(part == 0)
    def _():
        hbuf[...] = (_rms(x_ref[0], fn_ref[...]) * (1.0 + sc_ref[0]) + sh_ref[0]).astype(BF16)

    for j in range(FFN_SPLIT):
        @pl.when(part == j)
        def _(j=j):
            acc = _ffn_partial(hbuf[...], wgu_ref, wd_ref, range(n_ff * j // FFN_SPLIT, n_ff * (j + 1) // FFN_SPLIT))
            if j > 0:
                acc = acc + facc[...]
            if j < FFN_SPLIT - 1:
                facc[...] = acc
            else:
                y_ref[0] = _rms(x_ref[0] + g_ref[0] * acc, final_ref[...])

    qd = qd_ref[0]
    qm = qm_ref[0]
    qm_lat = qm[:, :MLA_KV_RANK]
    qm_pe = qm[:, MLA_KV_RANK:]

    def softmax_step(s, m, l):
        m_new = jnp.maximum(m, jnp.max(s, axis=-1, keepdims=True))
        alpha = jnp.exp2(m - m_new)
        p = jnp.exp2(s - m_new)
        return p.astype(BF16), alpha, m_new, alpha * l + jnp.sum(p, axis=-1, keepdims=True)

    def chunk_body(c, carry):
        md, ld, accd, mm, lm, accm = carry
        g = b * n_chunks + c
        slot = lax.rem(g, DECODE_SLOTS)
        ahead = g + (DECODE_SLOTS - 1)

        @pl.when(ahead < nb * n_chunks)
        def _():
            start(copies(lax.div(ahead, n_chunks), lax.rem(ahead, n_chunks), lax.rem(ahead, DECODE_SLOTS)))

        for cp in copies(b, c, slot):
            cp.wait()

        s_d = _dot(qd, kbuf[slot].astype(BF16))
        cb = cbuf[slot].astype(BF16)
        s_m = _nt_dot(qm_lat, cb) + _dot(qm_pe, pbuf[slot].astype(BF16))
        p_d, a_d, md, ld = softmax_step(s_d, md, ld)
        p_m, a_m, mm, lm = softmax_step(s_m, mm, lm)
        new_accd = []
        for h in range(N_DIFF_HEADS):
            rows = slice(heads_rows * h, heads_rows * (h + 1))
            vh = vbuf[slot, pl.ds(h, tok, stride=N_DIFF_HEADS), :].astype(BF16)
            new_accd.append(a_d[rows] * accd[h] + _dot(p_d, vh)[rows])
        accm = a_m * accm + _dot(p_m, cb)
        return md, ld, tuple(new_accd), mm, lm, accm

    init = (jnp.full((n_rows, 1), -jnp.inf, F32), jnp.zeros((n_rows, 1), F32),
            tuple(jnp.zeros((heads_rows, DIFF_VDIM), F32) for _ in range(N_DIFF_HEADS)),
            jnp.full((n_rows, 1), -jnp.inf, F32), jnp.zeros((n_rows, 1), F32),
            jnp.zeros((n_rows, MLA_KV_RANK), F32))
    md, ld, accd, mm, lm, accm = lax.fori_loop(0, n_chunks, chunk_body, init)

    n_new = ks_ref.shape[1]
    kpad[0:n_new, :] = ks_ref[0]
    vpad[0:n_new, :] = vs_ref[0]
    cpad[0:n_new, :] = cs_ref[0].astype(F32)
    qpos = lax.broadcasted_iota(jnp.int32, (n_rows, PAGE_SIZE), 0) % n_new
    kpos = lax.broadcasted_iota(jnp.int32, (n_rows, PAGE_SIZE), 1)
    ok = kpos <= qpos
    cself = cpad[...].astype(BF16)
    s_d = jnp.where(ok, _nt_dot(qd, kpad[...].astype(BF16)), -jnp.inf)
    s_m = jnp.where(ok, _nt_dot(qm, cself), -jnp.inf)
    p_d, a_d, md, ld = softmax_step(s_d, md, ld)
    p_m, a_m, mm, lm = softmax_step(s_m, mm, lm)
    pv = _dot(p_d, vpad[...].astype(BF16))
    accm = a_m * accm + _dot(p_m, cself[:, :MLA_KV_RANK])

    lam = lam_ref[...]
    half = heads_rows // 2
    for h in range(N_DIFF_HEADS):
        rows = slice(heads_rows * h, heads_rows * (h + 1))
        acc = a_d[rows] * accd[h] + pv[rows, DIFF_VDIM * h:DIFF_VDIM * (h + 1)]
        o = acc / ld[rows]
        od_ref[0, h] = o[:half] - lam * o[half:]
    om_ref[0] = accm / lm


def _decode_attn(page_table, lam, qd_blk, qm, k_self, v_self, kc_self, kt_pages, v_pages, c_pages, pe_pages,
                 x_ffn, mod_ffn, ffn_norm, w_gu, w_down, final_norm):
    nb, n_pages = page_table.shape
    assert x_ffn.shape[0] * FFN_SPLIT == nb and nb % mod_ffn.shape[0] == 0
    steps_per_group = nb // mod_ffn.shape[0]
    ffn_rows = x_ffn.shape[1]
    ffn_tile = pl.BlockSpec((1, ffn_rows, D_MODEL), lambda b, pt: (b // FFN_SPLIT, 0, 0))
    ffn_mod = lambda col: pl.BlockSpec((1, 1, D_MODEL), lambda b, pt: (b // steps_per_group, 0, col))
    n_rows = qd_blk.shape[1]
    n_new = k_self.shape[1]
    ch = PAGES_PER_CHUNK
    tok = ch * PAGE_SIZE
    row_blk = lambda w: pl.BlockSpec((1, n_rows, w), lambda b, pt: (b, 0, 0))
    new_blk = lambda w: pl.BlockSpec((1, n_new, w), lambda b, pt: (b, 0, 0))
    any_spec = pl.BlockSpec(memory_space=pl.ANY)
    grid_spec = pltpu.PrefetchScalarGridSpec(
        num_scalar_prefetch=1,
        grid=(nb,),
        in_specs=[pl.BlockSpec((1, LANES), lambda b, pt: (0, 0)),
                  row_blk(DQ_COLS), row_blk(QCAT), new_blk(DQ_COLS), new_blk(DQ_COLS), new_blk(QCAT),
                  any_spec, any_spec, any_spec, any_spec,
                  ffn_tile, ffn_mod(3), ffn_mod(4), ffn_mod(5),
                  _const_spec((1, D_MODEL)), _const_spec((D_MODEL, 2 * D_FF)), _const_spec((D_FF, D_MODEL)),
                  _const_spec((1, D_MODEL))],
        out_specs=[pl.BlockSpec((1, N_DIFF_HEADS, n_new, DIFF_VDIM), lambda b, pt: (b, 0, 0, 0)),
                   pl.BlockSpec((1, n_rows, MLA_KV_RANK), lambda b, pt: (b, 0, 0)),
                   ffn_tile],
        scratch_shapes=[pltpu.VMEM((DECODE_SLOTS, DQ_COLS, tok), F32),
                        pltpu.VMEM((DECODE_SLOTS, tok * N_DIFF_HEADS, DIFF_VDIM), F32),
                        pltpu.VMEM((DECODE_SLOTS, tok, MLA_KV_RANK), F32),
                        pltpu.VMEM((DECODE_SLOTS, LANES, tok), F32),
                        pltpu.SemaphoreType.DMA((4, DECODE_SLOTS)),
                        pltpu.VMEM((PAGE_SIZE, DQ_COLS), F32),
                        pltpu.VMEM((PAGE_SIZE, DQ_COLS), F32),
                        pltpu.VMEM((PAGE_SIZE, QCAT), F32),
                        pltpu.VMEM((ffn_rows, D_MODEL), BF16),
                        pltpu.VMEM((ffn_rows, D_MODEL), F32)])
    return pl.pallas_call(
        functools.partial(_decode_kernel, n_pages=n_pages),
        grid_spec=grid_spec,
        out_shape=[jax.ShapeDtypeStruct((nb, N_DIFF_HEADS, n_new, DIFF_VDIM), F32),
                   jax.ShapeDtypeStruct((nb, n_rows, MLA_KV_RANK), F32),
                   jax.ShapeDtypeStruct(x_ffn.shape, F32)],
        compiler_params=pltpu.CompilerParams(dimension_semantics=("arbitrary",),
                                             vmem_limit_bytes=VMEM_LIMIT),
        name="decode_attn",
    )(page_table, lam, qd_blk, qm, k_self, v_self, kc_self, kt_pages, v_pages, c_pages, pe_pages,
      x_ffn, mod_ffn, mod_ffn, mod_ffn, ffn_norm, w_gu, w_down, final_norm)


def _out_kernel(x_ref, g_ref, od_ref, om_ref, sub_ref, wuv_ref, wo_ref, o_ref):
    parts = []
    for h in range(N_DIFF_HEADS):
        o = od_ref[0, :, DIFF_VDIM * h:DIFF_VDIM * (h + 1)]
        parts.append((_rms(o, sub_ref[...]) * (1.0 - LAMBDA_INIT)).astype(BF16))
    pair = 2 * MLA_KV_RANK
    for j in range(N_MLA_HEADS // 2):
        parts.append(_dot(om_ref[0, :, pair * j:pair * (j + 1)], wuv_ref[j]).astype(BF16))
    mixed = jnp.concatenate(parts, axis=1)
    o_ref[0] = x_ref[0] + g_ref[0] * _dot(mixed, wo_ref[...])


def _out_proj(x, mod, od, om, diff_subln, w_uv_pairs, w_o, *, tm):
    nb, t, _ = x.shape
    r = mod.shape[1]
    rb = 1 if r == 1 else tm
    gate_idx = (lambda b, i: (b, 0, 2)) if r == 1 else (lambda b, i: (b, i, 2))
    tok = lambda w: pl.BlockSpec((1, tm, w), lambda b, i: (b, i, 0))
    return pl.pallas_call(
        _out_kernel,
        grid=(nb, t // tm),
        in_specs=[tok(D_MODEL), pl.BlockSpec((1, rb, D_MODEL), gate_idx),
                  tok(DIFF_WIDTH), tok(N_MLA_HEADS * MLA_KV_RANK),
                  _const_spec((1, DIFF_VDIM)),
                  _const_spec((N_MLA_HEADS // 2, 2 * MLA_KV_RANK, LANES)),
                  _const_spec((DIFF_WIDTH + MLA_WIDTH, D_MODEL))],
        out_specs=tok(D_MODEL),
        out_shape=jax.ShapeDtypeStruct((nb, t, D_MODEL), F32),
        compiler_params=pltpu.CompilerParams(dimension_semantics=("parallel", "parallel"),
                                             vmem_limit_bytes=VMEM_LIMIT),
        name="out_proj",
    )(x, mod, od, om, diff_subln, w_uv_pairs, w_o)


FF_CHUNK = 256


def _ffn_partial(h, wgu_ref, wd_ref, chunks):
    acc = None
    for c in chunks:
        lo = FF_CHUNK * c
        gate = _dot(h, wgu_ref[:, lo:lo + FF_CHUNK])
        up = _dot(h, wgu_ref[:, D_FF + lo:D_FF + lo + FF_CHUNK])
        act = (gate * jax.nn.sigmoid(gate) * up).astype(BF16)
        part = _dot(act, wd_ref[lo:lo + FF_CHUNK, :])
        acc = part if acc is None else acc + part
    return acc


def _ffn_tile(x, sh, sc, g, ffn_norm, wgu_ref, wd_ref, final_norm):
    h = (_rms(x, ffn_norm) * (1.0 + sc) + sh).astype(BF16)
    acc = _ffn_partial(h, wgu_ref, wd_ref, range(D_FF // FF_CHUNK))
    return _rms(x + g * acc, final_norm)


def _ffn_kernel(x_ref, sh_ref, sc_ref, g_ref, fn_ref, wgu_ref, wd_ref, final_ref, y_ref):
    y_ref[0] = _ffn_tile(x_ref[0], sh_ref[0], sc_ref[0], g_ref[0], fn_ref[...], wgu_ref, wd_ref, final_ref[...])


def _ffn(x, mod, ffn_norm, w_gu, w_down, final_norm, *, tm):
    nb, t, _ = x.shape
    r = mod.shape[1]
    rb = 1 if r == 1 else tm
    mod_idx = (lambda col: (lambda b, i: (b, 0, col))) if r == 1 else (lambda col: (lambda b, i: (b, i, col)))
    tok = pl.BlockSpec((1, tm, D_MODEL), lambda b, i: (b, i, 0))
    return pl.pallas_call(
        _ffn_kernel,
        grid=(nb, t // tm),
        in_specs=[tok,
                  pl.BlockSpec((1, rb, D_MODEL), mod_idx(3)),
                  pl.BlockSpec((1, rb, D_MODEL), mod_idx(4)),
                  pl.BlockSpec((1, rb, D_MODEL), mod_idx(5)),
                  _const_spec((1, D_MODEL)),
                  _const_spec((D_MODEL, 2 * D_FF)),
                  _const_spec((D_FF, D_MODEL)),
                  _const_spec((1, D_MODEL))],
        out_specs=tok,
        out_shape=jax.ShapeDtypeStruct((nb, t, D_MODEL), F32),
        compiler_params=pltpu.CompilerParams(dimension_semantics=("parallel", "parallel"),
                                             vmem_limit_bytes=VMEM_LIMIT),
        name="ffn",
    )(x, mod, mod, mod, ffn_norm, w_gu, w_down, final_norm)


def _rope_tables(pos, rot, period):
    half = rot // 2
    j = np.arange(LANES) % period
    first = j < half
    second = (j >= half) & (j < rot)
    freq_idx = np.where(first, j, np.clip(j - half, 0, half - 1)).astype(np.float32)
    inv = ROPE_THETA ** (-(jnp.asarray(freq_idx) * 2.0 / rot))
    ang = pos.astype(F32)[:, None] * inv[None, :]
    cos, sin = jnp.cos(ang), jnp.sin(ang)
    return jnp.stack([jnp.where(first | second, cos, 1.0),
                      jnp.where(first, -sin, 0.0),
                      jnp.where(second, sin, 0.0)])


def _prep_weights(w_in, w_uq, w_ukv, w_o, w_gate_up, w_down):
    w_in_p = jnp.pad(w_in, ((0, 0), (0, IN_COLS_PAD - IN_COLS))).astype(BF16)
    uq = w_uq.reshape(MLA_Q_RANK, N_MLA_HEADS, MLA_NOPE + MLA_ROPE)
    w_uq_p = jnp.concatenate([uq[:, :, :MLA_NOPE].reshape(MLA_Q_RANK, -1),
                              uq[:, :, MLA_NOPE:].reshape(MLA_Q_RANK, -1)], axis=1).astype(BF16)
    ukv = w_ukv.reshape(MLA_KV_RANK, N_MLA_HEADS, MLA_NOPE + MLA_V)
    uk_t = jnp.transpose(ukv[:, :, :MLA_NOPE], (1, 2, 0))
    uv = jnp.transpose(ukv[:, :, MLA_NOPE:], (1, 0, 2))
    zk = jnp.zeros_like(uk_t[0])
    zv = jnp.zeros_like(uv[0])
    uk_pairs = jnp.stack([jnp.block([[uk_t[2 * j], zk], [zk, uk_t[2 * j + 1]]])
                          for j in range(N_MLA_HEADS // 2)]).astype(BF16)
    uv_pairs = jnp.stack([jnp.block([[uv[2 * j], zv], [zv, uv[2 * j + 1]]])
                          for j in range(N_MLA_HEADS // 2)]).astype(BF16)
    return w_in_p, w_uq_p, uk_pairs, uv_pairs, w_o.astype(BF16), w_gate_up.astype(BF16), w_down.astype(BF16)


def kernel(x_prompt, x_sample, c_prompt, c_sample, cache_k_diff, cache_v_diff, cache_ckv, cache_kpe, page_table, mod_w, mod_b, attn_norm, w_in, q_norm, kv_norm, w_uq, w_ukv, lambda_q1, lambda_k1, lambda_q2, lambda_k2, diff_subln, w_o, ffn_norm, w_gate_up, w_down, final_norm):
    assert mod_w.shape[0] == 1, "single-layer trunk"
    bp, tp, _ = x_prompt.shape
    bs, ts, _ = x_sample.shape
    n_pool = cache_ckv.shape[1]
    n_pages = page_table.shape[1]
    assert n_pages % PAGES_PER_CHUNK == 0 and bs * (n_pages // PAGES_PER_CHUNK) >= DECODE_SLOTS - 1
    ns = bs * ts

    w_in_p, w_uq_p, uk_pairs, uv_pairs, w_o_b, w_gu_b, w_down_b = _prep_weights(
        w_in[0], w_uq[0], w_ukv[0], w_o[0], w_gate_up[0], w_down[0])
    final_norm2 = final_norm.reshape(1, D_MODEL)

    lam_vecs = jnp.concatenate([lambda_q1, lambda_k1, lambda_q2, lambda_k2], axis=0)
    mod_p, mod_s, lam = _modulation(c_prompt, jnp.repeat(c_sample, ts, axis=0), mod_w[0], mod_b, lam_vecs)
    mod_p = mod_p.reshape(bp, 1, 6 * D_MODEL)
    mod_s = mod_s.reshape(1, ns, 6 * D_MODEL)

    pos_p = jnp.arange(tp)
    pos_s = n_pages * PAGE_SIZE + (jnp.arange(ns) % ts)
    shared = (attn_norm, w_in_p, q_norm, kv_norm, w_uq_p, uk_pairs)

    tm_p = min(256, tp)
    dq, dk_f, dk_b, dv_f, dv_b, ckv_f, kpe_f, kcat, qcat = _in_proj(
        x_prompt, mod_p, _rope_tables(pos_p, DIFF_ROT, DIFF_DH), _rope_tables(pos_p, MLA_ROPE, MLA_ROPE),
        *shared, tm=tm_p)
    od_p, om_p = _prompt_attn(lam, dq, qcat, dk_b, dv_b, kcat, tq=tm_p)
    tm_f = min(512, tp)
    x1_p = _out_proj(x_prompt, mod_p, od_p, om_p, diff_subln, uv_pairs, w_o_b, tm=tm_f)
    ffn_rows = bp * tp * FFN_SPLIT // bs
    assert bp * tp * FFN_SPLIT % bs == 0 and tp % ffn_rows == 0 and ffn_rows % 8 == 0

    xs = x_sample.reshape(1, ns, D_MODEL)
    tm_s = min(256, ns)
    sq, sk_f, _, sv_f, _, sckv_f, skpe_f, skcat, sqcat = _in_proj(
        xs, mod_s, _rope_tables(pos_s, DIFF_ROT, DIFF_DH), _rope_tables(pos_s, MLA_ROPE, MLA_ROPE),
        *shared, tm=tm_s)
    n_hm = 2 * N_DIFF_HEADS
    sq5 = jnp.transpose(sq.reshape(bs, ts, n_hm, DIFF_DH), (0, 2, 1, 3))
    qd_blk = (sq5[:, :, :, None, :] * jnp.eye(n_hm, dtype=BF16)[None, :, None, :, None]
              ).reshape(bs, n_hm * ts, DQ_COLS)
    qm = jnp.transpose(sqcat.reshape(N_MLA_HEADS, bs, ts, QCAT), (1, 0, 2, 3)).reshape(bs, N_MLA_HEADS * ts, QCAT)
    kt_pages = jnp.transpose(cache_k_diff[0], (0, 2, 3, 4, 1)).reshape(n_pool, DQ_COLS, PAGE_SIZE)
    v_pages = cache_v_diff[0].reshape(n_pool, PAGE_SIZE * N_DIFF_HEADS, DIFF_VDIM)
    pe_pages = jnp.transpose(cache_kpe[0], (0, 2, 1))
    od_s, om_s, y_prompt = _decode_attn(page_table, lam, qd_blk, qm,
                                        sk_f.reshape(bs, ts, DQ_COLS), sv_f.reshape(bs, ts, DQ_COLS),
                                        skcat.reshape(bs, ts, QCAT), kt_pages, v_pages, cache_ckv[0], pe_pages,
                                        x1_p.reshape(bs // FFN_SPLIT, ffn_rows, D_MODEL), mod_p,
                                        ffn_norm, w_gu_b, w_down_b, final_norm2)
    y_prompt = y_prompt.reshape(bp, tp, D_MODEL)
    od_s = jnp.transpose(od_s, (0, 2, 1, 3)).reshape(1, ns, DIFF_WIDTH)
    om_s = jnp.transpose(om_s.reshape(bs, N_MLA_HEADS, ts, MLA_KV_RANK), (0, 2, 1, 3)
                         ).reshape(1, ns, N_MLA_HEADS * MLA_KV_RANK).astype(BF16)
    tm_fs = min(512, ns)
    x1_s = _out_proj(xs, mod_s, od_s, om_s, diff_subln, uv_pairs, w_o_b, tm=tm_fs)
    y_sample = _ffn(x1_s, mod_s, ffn_norm, w_gu_b, w_down_b, final_norm2, tm=tm_fs).reshape(bs, ts, D_MODEL)

    return (y_prompt, y_sample,
            dk_f.reshape(1, bp, tp, N_DIFF_HEADS, 2, DIFF_DH),
            dv_f.reshape(1, bp, tp, N_DIFF_HEADS, DIFF_VDIM),
            ckv_f.reshape(1, bp, tp, MLA_KV_RANK),
            kpe_f.reshape(1, bp, tp, MLA_ROPE),
            sk_f.reshape(1, bs, ts, N_DIFF_HEADS, 2, DIFF_DH),
            sv_f.reshape(1, bs, ts, N_DIFF_HEADS, DIFF_VDIM),
            sckv_f.reshape(1, bs, ts, MLA_KV_RANK),
            skpe_f.reshape(1, bs, ts, MLA_ROPE))
```

```python
import functools

import jax
import jax.numpy as jnp
import numpy as np
from jax import lax
from jax.experimental import pallas as pl
from jax.experimental.pallas import tpu as pltpu

F32 = jnp.float32
BF16 = jnp.bfloat16

D_MODEL = 1024
N_DIFF_HEADS = 4
DIFF_DH = 64
DIFF_VDIM = 2 * DIFF_DH
DIFF_ROT = DIFF_DH // 4
N_MLA_HEADS = 8
MLA_NOPE = 64
MLA_ROPE = 32
MLA_V = 64
MLA_Q_RANK = 384
MLA_KV_RANK = 256
DIFF_WIDTH = N_DIFF_HEADS * DIFF_VDIM
MLA_WIDTH = N_MLA_HEADS * MLA_V
DQ_COLS = N_DIFF_HEADS * 2 * DIFF_DH
IN_COLS = 3 * DQ_COLS + MLA_Q_RANK + MLA_KV_RANK + MLA_ROPE
D_FF = 2816
ROPE_THETA = 500000.0
NORM_EPS = 1e-6
PAGE_SIZE = 128
LOG2E = 1.4426950408889634
DIFF_SCALE = DIFF_DH ** -0.5 * LOG2E
MLA_SCALE = (MLA_NOPE + MLA_ROPE) ** -0.5 * LOG2E
LAMBDA_INIT = 0.8 - 0.6

LANES = 128
IN_COLS_PAD = 2304
QCAT = MLA_KV_RANK + LANES
PAGES_PER_CHUNK = 8
DECODE_SLOTS = 4
FFN_SPLIT = 4
VMEM_LIMIT = 56 * 1024 * 1024


def _nt_dot(a, b):
    return lax.dot_general(a, b, (((1,), (1,)), ((), ())), preferred_element_type=F32)


def _dot(a, b):
    return jnp.dot(a, b, preferred_element_type=F32)


def _rms(x, g):
    return x * lax.rsqrt(jnp.mean(x * x, axis=-1, keepdims=True) + NORM_EPS) * g


def _rope(x, tab_ref, half):
    return (x * tab_ref[0]
            + pltpu.roll(x, LANES - half, 1) * tab_ref[1]
            + pltpu.roll(x, half, 1) * tab_ref[2])


def _const_spec(shape):
    nd = len(shape)
    return pl.BlockSpec(shape, lambda *_: (0,) * nd, pipeline_mode=pl.Buffered(1))


def _mod_kernel(cp_ref, cs_ref, w_ref, b_ref, lv_ref, mp_ref, ms_ref, lam_ref):
    w = w_ref[...].astype(BF16)
    for c_ref, m_ref in ((cp_ref, mp_ref), (cs_ref, ms_ref)):
        c = c_ref[...]
        m_ref[...] = _dot((c * jax.nn.sigmoid(c)).astype(BF16), w) + b_ref[...]
    lv = lv_ref[...]
    s1 = jnp.sum(lv[0:1] * lv[1:2], axis=-1, keepdims=True)
    s2 = jnp.sum(lv[2:3] * lv[3:4], axis=-1, keepdims=True)
    lam_ref[...] = jnp.broadcast_to(jnp.exp(s1) - jnp.exp(s2) + LAMBDA_INIT, lam_ref.shape)


def _modulation(c_p, c_s, mod_w, mod_b, lam_vecs):
    n_p, n_s = c_p.shape[0], c_s.shape[0]
    tn = 1024
    return pl.pallas_call(
        _mod_kernel,
        grid=(6 * D_MODEL // tn,),
        in_specs=[pl.BlockSpec((n_p, D_MODEL), lambda j: (0, 0)),
                  pl.BlockSpec((n_s, D_MODEL), lambda j: (0, 0)),
                  pl.BlockSpec((D_MODEL, tn), lambda j: (0, j)),
                  pl.BlockSpec((1, tn), lambda j: (0, j)),
                  pl.BlockSpec((4, DIFF_DH), lambda j: (0, 0))],
        out_specs=[pl.BlockSpec((n_p, tn), lambda j: (0, j)),
                   pl.BlockSpec((n_s, tn), lambda j: (0, j)),
                   pl.BlockSpec((1, LANES), lambda j: (0, 0))],
        out_shape=[jax.ShapeDtypeStruct((n_p, 6 * D_MODEL), F32),
                   jax.ShapeDtypeStruct((n_s, 6 * D_MODEL), F32),
                   jax.ShapeDtypeStruct((1, LANES), F32)],
        compiler_params=pltpu.CompilerParams(dimension_semantics=("arbitrary",)),
        name="modulation",
    )(c_p, c_s, mod_w, mod_b, lam_vecs)


def _in_kernel(x_ref, sh_ref, sc_ref, rd_ref, rp_ref, rq_ref, an_ref, win_ref, qn_ref, kn_ref, wuq_ref, wk_ref,
               dq_ref, dkf_ref, dkb_ref, dvf_ref, dvb_ref, ckvf_ref, kpef_ref, *mla_refs, absorbed):
    x = x_ref[0]
    tm = x.shape[0]
    h = (_rms(x, an_ref[...]) * (1.0 + sc_ref[0]) + sh_ref[0]).astype(BF16)
    z = _dot(h, win_ref[...])
    for j in range(DQ_COLS // LANES):
        sl = slice(LANES * j, LANES * (j + 1))
        dq_ref[0, :, sl] = (_rope(z[:, sl], rd_ref, DIFF_ROT // 2) * DIFF_SCALE).astype(BF16)
        k = _rope(z[:, DQ_COLS + LANES * j:DQ_COLS + LANES * (j + 1)], rd_ref, DIFF_ROT // 2)
        dkf_ref[0, :, sl] = k
        dkb_ref[0, :, sl] = k.astype(BF16)
    dv = z[:, 2 * DQ_COLS:3 * DQ_COLS]
    for hh in range(N_DIFF_HEADS):
        dvf_ref[0, pl.ds(hh, tm, stride=N_DIFF_HEADS), :] = dv[:, DIFF_VDIM * hh:DIFF_VDIM * (hh + 1)]
    dvb_ref[0] = dv.astype(BF16)

    c0 = 3 * DQ_COLS
    c1 = c0 + MLA_Q_RANK
    cqn = _rms(z[:, c0:c0 + MLA_Q_RANK], qn_ref[...]).astype(BF16)
    ckvn = _rms(z[:, c1:c1 + MLA_KV_RANK], kn_ref[...])
    ckvf_ref[0] = ckvn
    kpe = _rope(z[:, c1 + MLA_KV_RANK:IN_COLS_PAD], rp_ref, MLA_ROPE // 2)
    kpef_ref[0] = kpe[:, :MLA_ROPE]
    q = _dot(cqn, wuq_ref[...])
    if not absorbed:
        qh_ref, kh_ref, vv_ref = mla_refs
        kv = _dot(ckvn.astype(BF16), wk_ref[...])
        kpe_at_rot = pltpu.roll(kpe, MLA_NOPE, 1)
        for hd in range(N_MLA_HEADS):
            sl = slice(LANES * hd, LANES * (hd + 1))
            qh_ref[0, :, sl] = (_rope(q[:, sl], rq_ref, MLA_ROPE // 2) * MLA_SCALE).astype(BF16)
            kh_ref[0, :, sl] = (kv[:, sl] + kpe_at_rot).astype(BF16)
        vv_ref[0] = kv[:, N_MLA_HEADS * LANES:].astype(BF16)
        return
    kcat_ref, qcat_ref = mla_refs
    nope_w = N_MLA_HEADS * MLA_NOPE
    qn = q[:, :nope_w].astype(BF16)
    for j in range(N_MLA_HEADS // 2):
        ql = _dot(qn[:, LANES * j:LANES * (j + 1)], wk_ref[j])
        qcat_ref[0, 2 * j, :, 0:MLA_KV_RANK] = (ql[:, :MLA_KV_RANK] * MLA_SCALE).astype(BF16)
        qcat_ref[0, 2 * j + 1, :, 0:MLA_KV_RANK] = (ql[:, MLA_KV_RANK:] * MLA_SCALE).astype(BF16)
    lane = lax.broadcasted_iota(jnp.int32, (tm, LANES), 1)
    heads_per_slab = LANES // MLA_ROPE
    for g in range(N_MLA_HEADS // heads_per_slab):
        qp = _rope(q[:, nope_w + LANES * g:nope_w + LANES * (g + 1)], rp_ref, MLA_ROPE // 2) * MLA_SCALE
        for i in range(heads_per_slab):
            moved = qp if i == 0 else pltpu.roll(qp, LANES - MLA_ROPE * i, 1)
            qcat_ref[0, heads_per_slab * g + i, :, MLA_KV_RANK:QCAT] = (
                jnp.where(lane < MLA_ROPE, moved, 0.0).astype(BF16))

    kcat_ref[0, :, 0:MLA_KV_RANK] = ckvn.astype(BF16)
    kcat_ref[0, :, MLA_KV_RANK:QCAT] = kpe.astype(BF16)


def _in_proj(x, mod, rope_d, rope_p, rope_q, attn_norm, w_in_p, q_norm, kv_norm, w_q, w_k, *, tm, absorbed):
    nb, t, _ = x.shape
    r = mod.shape[1]
    rb = 1 if r == 1 else tm
    mod_idx = (lambda col: (lambda b, i: (b, 0, col))) if r == 1 else (lambda col: (lambda b, i: (b, i, col)))
    tok = lambda w: pl.BlockSpec((1, tm, w), lambda b, i: (b, i, 0))
    rope_spec = pl.BlockSpec((3, tm, LANES), lambda b, i: (0, i, 0))
    out_shape = [jax.ShapeDtypeStruct((nb, t, DQ_COLS), BF16),
                 jax.ShapeDtypeStruct((nb, t, DQ_COLS), F32), jax.ShapeDtypeStruct((nb, t, DQ_COLS), BF16),
                 jax.ShapeDtypeStruct((nb, t * N_DIFF_HEADS, DIFF_VDIM), F32),
                 jax.ShapeDtypeStruct((nb, t, DQ_COLS), BF16),
                 jax.ShapeDtypeStruct((nb, t, MLA_KV_RANK), F32),
                 jax.ShapeDtypeStruct((nb, t, MLA_ROPE), F32)]
    out_specs = [tok(DQ_COLS), tok(DQ_COLS), tok(DQ_COLS),
                 pl.BlockSpec((1, tm * N_DIFF_HEADS, DIFF_VDIM), lambda b, i: (b, i, 0)), tok(DQ_COLS),
                 tok(MLA_KV_RANK), tok(MLA_ROPE)]
    if absorbed:
        out_shape += [jax.ShapeDtypeStruct((nb, t, QCAT), BF16),
                      jax.ShapeDtypeStruct((nb, N_MLA_HEADS, t, QCAT), BF16)]
        out_specs += [tok(QCAT), pl.BlockSpec((1, N_MLA_HEADS, tm, QCAT), lambda b, i: (b, 0, i, 0))]
    else:
        out_shape += [jax.ShapeDtypeStruct((nb, t, N_MLA_HEADS * LANES), BF16),
                      jax.ShapeDtypeStruct((nb, t, N_MLA_HEADS * LANES), BF16),
                      jax.ShapeDtypeStruct((nb, t, MLA_WIDTH), BF16)]
        out_specs += [tok(N_MLA_HEADS * LANES), tok(N_MLA_HEADS * LANES), tok(MLA_WIDTH)]
    return pl.pallas_call(
        functools.partial(_in_kernel, absorbed=absorbed),
        grid=(nb, t // tm),
        in_specs=[tok(D_MODEL),
                  pl.BlockSpec((1, rb, D_MODEL), mod_idx(0)),
                  pl.BlockSpec((1, rb, D_MODEL), mod_idx(1)),
                  rope_spec, rope_spec, rope_spec,
                  _const_spec((1, D_MODEL)),
                  _const_spec((D_MODEL, IN_COLS_PAD)),
                  _const_spec((1, MLA_Q_RANK)),
                  _const_spec((1, MLA_KV_RANK)),
                  _const_spec(w_q.shape),
                  _const_spec(w_k.shape)],
        out_specs=out_specs,
        out_shape=out_shape,
        compiler_params=pltpu.CompilerParams(dimension_semantics=("parallel", "parallel"),
                                             vmem_limit_bytes=VMEM_LIMIT),
        name="in_proj",
    )(x, mod, mod, rope_d, rope_p, rope_q, attn_norm, w_in_p, q_norm, kv_norm, w_q, w_k)


def _online_update(s, m_ref, l_ref, acc_ref, v, idx):
    blocks = [s[:, LANES * j:LANES * (j + 1)] for j in range(s.shape[1] // LANES)]
    m_old = m_ref[idx]
    m_new = jnp.maximum(m_old, jnp.max(functools.reduce(jnp.maximum, blocks), axis=-1, keepdims=True))
    alpha = jnp.exp2(m_old - m_new)
    p = [jnp.exp2(blk - m_new) for blk in blocks]
    l_ref[idx] = alpha * l_ref[idx] + functools.reduce(jnp.add, p)
    pv = _dot(jnp.concatenate(p, axis=1).astype(BF16), v)
    acc = acc_ref[idx]
    scale = jnp.concatenate([alpha] * (acc.shape[1] // LANES), axis=1)
    acc_ref[idx] = scale * acc + pv
    m_ref[idx] = m_new


def _prompt_attn_kernel(lam_ref, dq_ref, qh_ref, dk_ref, dv_ref, kh_ref, vv_ref, od_ref, om_ref,
                        qd_s, md_s, ld_s, accd_s, mm_s, lm_s, accm_s, *, tq):
    qi = pl.program_id(1)
    lane = lax.broadcasted_iota(jnp.int32, (tq, LANES), 1)
    for h in range(N_DIFF_HEADS):
        qh = dq_ref[0, :, LANES * h:LANES * (h + 1)]
        qd_s[h, 0:tq, :] = jnp.where(lane < DIFF_DH, qh, jnp.zeros_like(qh))
        qd_s[h, tq:2 * tq, :] = jnp.where(lane >= DIFF_DH, qh, jnp.zeros_like(qh))
    md_s[...] = jnp.full(md_s.shape, -jnp.inf, F32)
    ld_s[...] = jnp.zeros(ld_s.shape, F32)
    accd_s[...] = jnp.zeros(accd_s.shape, F32)
    mm_s[...] = jnp.full(mm_s.shape, -jnp.inf, F32)
    lm_s[...] = jnp.zeros(lm_s.shape, F32)
    accm_s[...] = jnp.zeros(accm_s.shape, F32)

    def key_tile(kt, diagonal):
        k0 = pl.multiple_of(kt * tq, tq)

        def causal(s):
            if not diagonal:
                return s
            r = lax.broadcasted_iota(jnp.int32, s.shape, 0) % tq
            c = lax.broadcasted_iota(jnp.int32, s.shape, 1)
            return jnp.where(c <= r, s, -jnp.inf)

        for h in range(N_DIFF_HEADS):
            sl = slice(LANES * h, LANES * (h + 1))
            s = causal(_nt_dot(qd_s[h], dk_ref[0, pl.ds(k0, tq), sl]))
            _online_update(s, md_s, ld_s, accd_s, dv_ref[0, pl.ds(k0, tq), sl], h)
        for hd in range(N_MLA_HEADS):
            sl = slice(LANES * hd, LANES * (hd + 1))
            s = causal(_nt_dot(qh_ref[0, :, sl], kh_ref[0, pl.ds(k0, tq), sl]))
            pair = slice(LANES * (hd // 2), LANES * (hd // 2 + 1))
            _online_update(s, mm_s, lm_s, accm_s, vv_ref[0, pl.ds(k0, tq), pair], hd)

    def body(kt, carry):
        key_tile(kt, False)
        return carry

    lax.fori_loop(0, qi, body, 0)
    key_tile(qi, True)

    lam = lam_ref[...]
    for h in range(N_DIFF_HEADS):
        o = accd_s[h] / jnp.sum(ld_s[h], axis=-1, keepdims=True)
        od_ref[0, :, LANES * h:LANES * (h + 1)] = o[:tq] - lam * o[tq:]
    for j in range(N_MLA_HEADS // 2):
        even = accm_s[2 * j] / jnp.sum(lm_s[2 * j], axis=-1, keepdims=True)
        odd = accm_s[2 * j + 1] / jnp.sum(lm_s[2 * j + 1], axis=-1, keepdims=True)
        om_ref[0, :, LANES * j:LANES * (j + 1)] = jnp.where(lane < MLA_V, even, odd).astype(BF16)


def _prompt_attn(lam, dq, qh, dk, dv, kh, vv, *, tq):
    b, t, _ = dq.shape
    kv = lambda w: pl.BlockSpec((1, t, w), lambda bi, qi: (bi, 0, 0))
    return pl.pallas_call(
        functools.partial(_prompt_attn_kernel, tq=tq),
        grid=(b, t // tq),
        in_specs=[pl.BlockSpec((1, LANES), lambda bi, qi: (0, 0)),
                  pl.BlockSpec((1, tq, DQ_COLS), lambda bi, qi: (bi, qi, 0)),
                  pl.BlockSpec((1, tq, N_MLA_HEADS * LANES), lambda bi, qi: (bi, qi, 0)),
                  kv(DQ_COLS), kv(DQ_COLS), kv(N_MLA_HEADS * LANES), kv(MLA_WIDTH)],
        out_specs=[pl.BlockSpec((1, tq, DIFF_WIDTH), lambda bi, qi: (bi, qi, 0)),
                   pl.BlockSpec((1, tq, MLA_WIDTH), lambda bi, qi: (bi, qi, 0))],
        out_shape=[jax.ShapeDtypeStruct((b, t, DIFF_WIDTH), F32),
                   jax.ShapeDtypeStruct((b, t, MLA_WIDTH), BF16)],
        scratch_shapes=[pltpu.VMEM((N_DIFF_HEADS, 2 * tq, LANES), BF16),
                        pltpu.VMEM((N_DIFF_HEADS, 2 * tq, LANES), F32),
                        pltpu.VMEM((N_DIFF_HEADS, 2 * tq, LANES), F32),
                        pltpu.VMEM((N_DIFF_HEADS, 2 * tq, DIFF_VDIM), F32),
                        pltpu.VMEM((N_MLA_HEADS, tq, LANES), F32),
                        pltpu.VMEM((N_MLA_HEADS, tq, LANES), F32),
                        pltpu.VMEM((N_MLA_HEADS, tq, LANES), F32)],
        compiler_params=pltpu.CompilerParams(dimension_semantics=("parallel", "arbitrary"),
                                             vmem_limit_bytes=VMEM_LIMIT),
        name="prompt_attn",
    )(lam, dq, qh, dk, dv, kh, vv)


def _decode_kernel(pt_ref, lam_ref, qd_ref, qm_ref, ks_ref, vs_ref, cs_ref,
                   kt_hbm, v_hbm, c_hbm, pe_hbm,
                   x_ref, sh_ref, sc_ref, g_ref, fn_ref, wgu_ref, wd_ref, final_ref,
                   od_ref, om_ref, y_ref,
                   kbuf, vbuf, cbuf, pbuf, sems, kpad, vpad, cpad, hbuf, facc, *, n_pages):
    ch = PAGES_PER_CHUNK
    n_chunks = n_pages // ch
    b = pl.program_id(0)
    nb = pl.num_programs(0)
    tok = ch * PAGE_SIZE
    n_rows = qd_ref.shape[1]
    heads_rows = n_rows // N_DIFF_HEADS

    def copies(bb, c, slot):
        out = []
        for j in range(ch):
            pg = pt_ref[bb, c * ch + j]
            out.append(pltpu.make_async_copy(kt_hbm.at[pg], kbuf.at[slot, :, pl.ds(j * PAGE_SIZE, PAGE_SIZE)],
                                             sems.at[0, slot]))
            out.append(pltpu.make_async_copy(v_hbm.at[pg], vbuf.at[slot, pl.ds(j * PAGE_SIZE * N_DIFF_HEADS,
                                                                              PAGE_SIZE * N_DIFF_HEADS), :],
                                             sems.at[1, slot]))
            out.append(pltpu.make_async_copy(c_hbm.at[pg], cbuf.at[slot, pl.ds(j * PAGE_SIZE, PAGE_SIZE), :],
                                             sems.at[2, slot]))
            out.append(pltpu.make_async_copy(pe_hbm.at[pg],
                                             pbuf.at[slot, pl.ds(0, MLA_ROPE), pl.ds(j * PAGE_SIZE, PAGE_SIZE)],
                                             sems.at[3, slot]))
        return out

    def start(cps):
        for i, cp in enumerate(cps):
            cp.start(priority=i % 2)

    @pl.when(b == 0)
    def _():
        pbuf[...] = jnp.zeros(pbuf.shape, F32)
        kpad[...] = jnp.zeros(kpad.shape, F32)
        vpad[...] = jnp.zeros(vpad.shape, F32)
        cpad[...] = jnp.zeros(cpad.shape, F32)
        for d in range(DECODE_SLOTS - 1):
            start(copies(d // n_chunks, d % n_chunks, d))

    part = lax.rem(b, FFN_SPLIT)
    n_ff = D_FF // FF_CHUNK

    @pl.when(part == 0)
    def _():
        hbuf[...] = (_rms(x_ref[0], fn_ref[...]) * (1.0 + sc_ref[0]) + sh_ref[0]).astype(BF16)

    for j in range(FFN_SPLIT):
        @pl.when(part == j)
        def _(j=j):
            acc = _ffn_partial(hbuf[...], wgu_ref, wd_ref, range(n_ff * j // FFN_SPLIT, n_ff * (j + 1) // FFN_SPLIT))
            if j > 0:
                acc = acc + facc[...]
            if j < FFN_SPLIT - 1:
                facc[...] = acc
            else:
                y_ref[0] = _rms(x_ref[0] + g_ref[0] * acc, final_ref[...])

    qd = qd_ref[0]
    qm = qm_ref[0]
    qm_lat = qm[:, :MLA_KV_RANK]
    qm_pe = qm[:, MLA_KV_RANK:]

    def softmax_step(s, m, l):
        m_new = jnp.maximum(m, jnp.max(s, axis=-1, keepdims=True))
        alpha = jnp.exp2(m - m_new)
        p = jnp.exp2(s - m_new)
        return p.astype(BF16), alpha, m_new, alpha * l + jnp.sum(p, axis=-1, keepdims=True)

    def chunk_body(c, carry):
        md, ld, accd, mm, lm, accm = carry
        g = b * n_chunks + c
        slot = lax.rem(g, DECODE_SLOTS)
        ahead = g + (DECODE_SLOTS - 1)

        @pl.when(ahead < nb * n_chunks)
        def _():
            start(copies(lax.div(ahead, n_chunks), lax.rem(ahead, n_chunks), lax.rem(ahead, DECODE_SLOTS)))

        for cp in copies(b, c, slot):
            cp.wait()

        s_d = _dot(qd, kbuf[slot].astype(BF16))
        cb = cbuf[slot].astype(BF16)
        s_m = _nt_dot(qm_lat, cb) + _dot(qm_pe, pbuf[slot].astype(BF16))
        p_d, a_d, md, ld = softmax_step(s_d, md, ld)
        p_m, a_m, mm, lm = softmax_step(s_m, mm, lm)
        new_accd = []
        for h in range(N_DIFF_HEADS):
            rows = slice(heads_rows * h, heads_rows * (h + 1))
            vh = vbuf[slot, pl.ds(h, tok, stride=N_DIFF_HEADS), :].astype(BF16)
            new_accd.append(a_d[rows] * accd[h] + _dot(p_d, vh)[rows])
        accm = a_m * accm + _dot(p_m, cb)
        return md, ld, tuple(new_accd), mm, lm, accm

    init = (jnp.full((n_rows, 1), -jnp.inf, F32), jnp.zeros((n_rows, 1), F32),
            tuple(jnp.zeros((heads_rows, DIFF_VDIM), F32) for _ in range(N_DIFF_HEADS)),
            jnp.full((n_rows, 1), -jnp.inf, F32), jnp.zeros((n_rows, 1), F32),
            jnp.zeros((n_rows, MLA_KV_RANK), F32))
    md, ld, accd, mm, lm, accm = lax.fori_loop(0, n_chunks, chunk_body, init)

    n_new = ks_ref.shape[1]
    kpad[0:n_new, :] = ks_ref[0]
    vpad[0:n_new, :] = vs_ref[0]
    cpad[0:n_new, :] = cs_ref[0].astype(F32)
    qpos = lax.broadcasted_iota(jnp.int32, (n_rows, PAGE_SIZE), 0) % n_new
    kpos = lax.broadcasted_iota(jnp.int32, (n_rows, PAGE_SIZE), 1)
    ok = kpos <= qpos
    cself = cpad[...].astype(BF16)
    s_d = jnp.where(ok, _nt_dot(qd, kpad[...].astype(BF16)), -jnp.inf)
    s_m = jnp.where(ok, _nt_dot(qm, cself), -jnp.inf)
    p_d, a_d, md, ld = softmax_step(s_d, md, ld)
    p_m, a_m, mm, lm = softmax_step(s_m, mm, lm)
    pv = _dot(p_d, vpad[...].astype(BF16))
    accm = a_m * accm + _dot(p_m, cself[:, :MLA_KV_RANK])

    lam = lam_ref[...]
    half = heads_rows // 2
    for h in range(N_DIFF_HEADS):
        rows = slice(heads_rows * h, heads_rows * (h + 1))
        acc = a_d[rows] * accd[h] + pv[rows, DIFF_VDIM * h:DIFF_VDIM * (h + 1)]
        o = acc / ld[rows]
        od_ref[0, h] = o[:half] - lam * o[half:]
    om_ref[0] = accm / lm


def _decode_attn(page_table, lam, qd_blk, qm, k_self, v_self, kc_self, kt_pages, v_pages, c_pages, pe_pages,
                 x_ffn, mod_ffn, ffn_norm, w_gu, w_down, final_norm):
    nb, n_pages = page_table.shape
    assert x_ffn.shape[0] * FFN_SPLIT == nb and nb % mod_ffn.shape[0] == 0
    steps_per_group = nb // mod_ffn.shape[0]
    ffn_rows = x_ffn.shape[1]
    ffn_tile = pl.BlockSpec((1, ffn_rows, D_MODEL), lambda b, pt: (b // FFN_SPLIT, 0, 0))
    ffn_mod = lambda col: pl.BlockSpec((1, 1, D_MODEL), lambda b, pt: (b // steps_per_group, 0, col))
    n_rows = qd_blk.shape[1]
    n_new = k_self.shape[1]
    ch = PAGES_PER_CHUNK
    tok = ch * PAGE_SIZE
    row_blk = lambda w: pl.BlockSpec((1, n_rows, w), lambda b, pt: (b, 0, 0))
    new_blk = lambda w: pl.BlockSpec((1, n_new, w), lambda b, pt: (b, 0, 0))
    any_spec = pl.BlockSpec(memory_space=pl.ANY)
    grid_spec = pltpu.PrefetchScalarGridSpec(
        num_scalar_prefetch=1,
        grid=(nb,),
        in_specs=[pl.BlockSpec((1, LANES), lambda b, pt: (0, 0)),
                  row_blk(DQ_COLS), row_blk(QCAT), new_blk(DQ_COLS), new_blk(DQ_COLS), new_blk(QCAT),
                  any_spec, any_spec, any_spec, any_spec,
                  ffn_tile, ffn_mod(3), ffn_mod(4), ffn_mod(5),
                  _const_spec((1, D_MODEL)), _const_spec((D_MODEL, 2 * D_FF)), _const_spec((D_FF, D_MODEL)),
                  _const_spec((1, D_MODEL))],
        out_specs=[pl.BlockSpec((1, N_DIFF_HEADS, n_new, DIFF_VDIM), lambda b, pt: (b, 0, 0, 0)),
                   pl.BlockSpec((1, n_rows, MLA_KV_RANK), lambda b, pt: (b, 0, 0)),
                   ffn_tile],
        scratch_shapes=[pltpu.VMEM((DECODE_SLOTS, DQ_COLS, tok), F32),
                        pltpu.VMEM((DECODE_SLOTS, tok * N_DIFF_HEADS, DIFF_VDIM), F32),
                        pltpu.VMEM((DECODE_SLOTS, tok, MLA_KV_RANK), F32),
                        pltpu.VMEM((DECODE_SLOTS, LANES, tok), F32),
                        pltpu.SemaphoreType.DMA((4, DECODE_SLOTS)),
                        pltpu.VMEM((PAGE_SIZE, DQ_COLS), F32),
                        pltpu.VMEM((PAGE_SIZE, DQ_COLS), F32),
                        pltpu.VMEM((PAGE_SIZE, QCAT), F32),
                        pltpu.VMEM((ffn_rows, D_MODEL), BF16),
                        pltpu.VMEM((ffn_rows, D_MODEL), F32)])
    return pl.pallas_call(
        functools.partial(_decode_kernel, n_pages=n_pages),
        grid_spec=grid_spec,
        out_shape=[jax.ShapeDtypeStruct((nb, N_DIFF_HEADS, n_new, DIFF_VDIM), F32),
                   jax.ShapeDtypeStruct((nb, n_rows, MLA_KV_RANK), F32),
                   jax.ShapeDtypeStruct(x_ffn.shape, F32)],
        compiler_params=pltpu.CompilerParams(dimension_semantics=("arbitrary",),
                                             vmem_limit_bytes=VMEM_LIMIT),
        name="decode_attn",
    )(page_table, lam, qd_blk, qm, k_self, v_self, kc_self, kt_pages, v_pages, c_pages, pe_pages,
      x_ffn, mod_ffn, mod_ffn, mod_ffn, ffn_norm, w_gu, w_down, final_norm)


def _out_kernel(x_ref, g_ref, od_ref, om_ref, sub_ref, wuv_ref, wo_ref, o_ref):
    parts = []
    for h in range(N_DIFF_HEADS):
        o = od_ref[0, :, DIFF_VDIM * h:DIFF_VDIM * (h + 1)]
        parts.append((_rms(o, sub_ref[...]) * (1.0 - LAMBDA_INIT)).astype(BF16))
    if om_ref.shape[-1] == MLA_WIDTH:
        parts.append(om_ref[0])
    else:
        pair = 2 * MLA_KV_RANK
        for j in range(N_MLA_HEADS // 2):
            parts.append(_dot(om_ref[0, :, pair * j:pair * (j + 1)], wuv_ref[j]).astype(BF16))
    mixed = jnp.concatenate(parts, axis=1)
    o_ref[0] = x_ref[0] + g_ref[0] * _dot(mixed, wo_ref[...])


def _out_proj(x, mod, od, om, diff_subln, w_uv_pairs, w_o, *, tm):
    nb, t, _ = x.shape
    r = mod.shape[1]
    rb = 1 if r == 1 else tm
    gate_idx = (lambda b, i: (b, 0, 2)) if r == 1 else (lambda b, i: (b, i, 2))
    tok = lambda w: pl.BlockSpec((1, tm, w), lambda b, i: (b, i, 0))
    return pl.pallas_call(
        _out_kernel,
        grid=(nb, t // tm),
        in_specs=[tok(D_MODEL), pl.BlockSpec((1, rb, D_MODEL), gate_idx),
                  tok(DIFF_WIDTH), tok(om.shape[-1]),
                  _const_spec((1, DIFF_VDIM)),
                  _const_spec((N_MLA_HEADS // 2, 2 * MLA_KV_RANK, LANES)),
                  _const_spec((DIFF_WIDTH + MLA_WIDTH, D_MODEL))],
        out_specs=tok(D_MODEL),
        out_shape=jax.ShapeDtypeStruct((nb, t, D_MODEL), F32),
        compiler_params=pltpu.CompilerParams(dimension_semantics=("parallel", "parallel"),
                                             vmem_limit_bytes=VMEM_LIMIT),
        name="out_proj",
    )(x, mod, od, om, diff_subln, w_uv_pairs, w_o)


FF_CHUNK = 256


def _ffn_partial(h, wgu_ref, wd_ref, chunks):
    acc = None
    for c in chunks:
        lo = FF_CHUNK * c
        gate = _dot(h, wgu_ref[:, lo:lo + FF_CHUNK])
        up = _dot(h, wgu_ref[:, D_FF + lo:D_FF + lo + FF_CHUNK])
        act = (gate * jax.nn.sigmoid(gate) * up).astype(BF16)
        part = _dot(act, wd_ref[lo:lo + FF_CHUNK, :])
        acc = part if acc is None else acc + part
    return acc


def _ffn_tile(x, sh, sc, g, ffn_norm, wgu_ref, wd_ref, final_norm):
    h = (_rms(x, ffn_norm) * (1.0 + sc) + sh).astype(BF16)
    acc = _ffn_partial(h, wgu_ref, wd_ref, range(D_FF // FF_CHUNK))
    return _rms(x + g * acc, final_norm)


def _ffn_kernel(x_ref, sh_ref, sc_ref, g_ref, fn_ref, wgu_ref, wd_ref, final_ref, y_ref):
    y_ref[0] = _ffn_tile(x_ref[0], sh_ref[0], sc_ref[0], g_ref[0], fn_ref[...], wgu_ref, wd_ref, final_ref[...])


def _ffn(x, mod, ffn_norm, w_gu, w_down, final_norm, *, tm):
    nb, t, _ = x.shape
    r = mod.shape[1]
    rb = 1 if r == 1 else tm
    mod_idx = (lambda col: (lambda b, i: (b, 0, col))) if r == 1 else (lambda col: (lambda b, i: (b, i, col)))
    tok = pl.BlockSpec((1, tm, D_MODEL), lambda b, i: (b, i, 0))
    return pl.pallas_call(
        _ffn_kernel,
        grid=(nb, t // tm),
        in_specs=[tok,
                  pl.BlockSpec((1, rb, D_MODEL), mod_idx(3)),
                  pl.BlockSpec((1, rb, D_MODEL), mod_idx(4)),
                  pl.BlockSpec((1, rb, D_MODEL), mod_idx(5)),
                  _const_spec((1, D_MODEL)),
                  _const_spec((D_MODEL, 2 * D_FF)),
                  _const_spec((D_FF, D_MODEL)),
                  _const_spec((1, D_MODEL))],
        out_specs=tok,
        out_shape=jax.ShapeDtypeStruct((nb, t, D_MODEL), F32),
        compiler_params=pltpu.CompilerParams(dimension_semantics=("parallel", "parallel"),
                                             vmem_limit_bytes=VMEM_LIMIT),
        name="ffn",
    )(x, mod, mod, mod, ffn_norm, w_gu, w_down, final_norm)


def _rope_tables(pos, rot, period, offset=0):
    half = rot // 2
    j = (np.arange(LANES) - offset) % period
    first = j < half
    second = (j >= half) & (j < rot)
    freq_idx = np.where(first, j, np.clip(j - half, 0, half - 1)).astype(np.float32)
    inv = ROPE_THETA ** (-(jnp.asarray(freq_idx) * 2.0 / rot))
    ang = pos.astype(F32)[:, None] * inv[None, :]
    cos, sin = jnp.cos(ang), jnp.sin(ang)
    return jnp.stack([jnp.where(first | second, cos, 1.0),
                      jnp.where(first, -sin, 0.0),
                      jnp.where(second, sin, 0.0)])


def _prep_weights(w_in, w_uq, w_ukv, w_o, w_gate_up, w_down):
    w_in_p = jnp.pad(w_in, ((0, 0), (0, IN_COLS_PAD - IN_COLS))).astype(BF16)
    uq = w_uq.reshape(MLA_Q_RANK, N_MLA_HEADS, MLA_NOPE + MLA_ROPE)
    w_uq_p = jnp.concatenate([uq[:, :, :MLA_NOPE].reshape(MLA_Q_RANK, -1),
                              uq[:, :, MLA_NOPE:].reshape(MLA_Q_RANK, -1)], axis=1).astype(BF16)
    ukv = w_ukv.reshape(MLA_KV_RANK, N_MLA_HEADS, MLA_NOPE + MLA_V)
    uk_t = jnp.transpose(ukv[:, :, :MLA_NOPE], (1, 2, 0))
    uv = jnp.transpose(ukv[:, :, MLA_NOPE:], (1, 0, 2))
    zk = jnp.zeros_like(uk_t[0])
    zv = jnp.zeros_like(uv[0])
    uk_pairs = jnp.stack([jnp.block([[uk_t[2 * j], zk], [zk, uk_t[2 * j + 1]]])
                          for j in range(N_MLA_HEADS // 2)]).astype(BF16)
    uv_pairs = jnp.stack([jnp.block([[uv[2 * j], zv], [zv, uv[2 * j + 1]]])
                          for j in range(N_MLA_HEADS // 2)]).astype(BF16)
    pad_q = jnp.zeros((MLA_Q_RANK, N_MLA_HEADS, LANES - MLA_NOPE - MLA_ROPE), w_uq.dtype)
    w_uq_h = jnp.concatenate([uq, pad_q], axis=2).reshape(MLA_Q_RANK, N_MLA_HEADS * LANES).astype(BF16)
    pad_k = jnp.zeros((MLA_KV_RANK, N_MLA_HEADS, LANES - MLA_NOPE), w_ukv.dtype)
    w_kv = jnp.concatenate([jnp.concatenate([ukv[:, :, :MLA_NOPE], pad_k], axis=2).reshape(MLA_KV_RANK, -1),
                            ukv[:, :, MLA_NOPE:].reshape(MLA_KV_RANK, -1)], axis=1).astype(BF16)
    return (w_in_p, w_uq_p, uk_pairs, uv_pairs, w_uq_h, w_kv,
            w_o.astype(BF16), w_gate_up.astype(BF16), w_down.astype(BF16))


def kernel(x_prompt, x_sample, c_prompt, c_sample, cache_k_diff, cache_v_diff, cache_ckv, cache_kpe, page_table, mod_w, mod_b, attn_norm, w_in, q_norm, kv_norm, w_uq, w_ukv, lambda_q1, lambda_k1, lambda_q2, lambda_k2, diff_subln, w_o, ffn_norm, w_gate_up, w_down, final_norm):
    assert mod_w.shape[0] == 1, "single-layer trunk"
    bp, tp, _ = x_prompt.shape
    bs, ts, _ = x_sample.shape
    n_pool = cache_ckv.shape[1]
    n_pages = page_table.shape[1]
    assert n_pages % PAGES_PER_CHUNK == 0 and bs * (n_pages // PAGES_PER_CHUNK) >= DECODE_SLOTS - 1
    ns = bs * ts

    w_in_p, w_uq_p, uk_pairs, uv_pairs, w_uq_h, w_kv, w_o_b, w_gu_b, w_down_b = _prep_weights(
        w_in[0], w_uq[0], w_ukv[0], w_o[0], w_gate_up[0], w_down[0])
    final_norm2 = final_norm.reshape(1, D_MODEL)

    lam_vecs = jnp.concatenate([lambda_q1, lambda_k1, lambda_q2, lambda_k2], axis=0)
    mod_p, mod_s, lam = _modulation(c_prompt, jnp.repeat(c_sample, ts, axis=0), mod_w[0], mod_b, lam_vecs)
    mod_p = mod_p.reshape(bp, 1, 6 * D_MODEL)
    mod_s = mod_s.reshape(1, ns, 6 * D_MODEL)

    pos_p = jnp.arange(tp)
    pos_s = n_pages * PAGE_SIZE + (jnp.arange(ns) % ts)
    shared = (attn_norm, w_in_p, q_norm, kv_norm)

    tm_p = min(256, tp)
    dq, dk_f, dk_b, dv_f, dv_b, ckv_f, kpe_f, qh, kh, vv = _in_proj(
        x_prompt, mod_p, _rope_tables(pos_p, DIFF_ROT, DIFF_DH), _rope_tables(pos_p, MLA_ROPE, MLA_ROPE),
        _rope_tables(pos_p, MLA_ROPE, LANES, offset=MLA_NOPE), *shared, w_uq_h, w_kv, tm=tm_p, absorbed=False)
    od_p, om_p = _prompt_attn(lam, dq, qh, dk_b, dv_b, kh, vv, tq=tm_p)
    tm_f = min(512, tp)
    x1_p = _out_proj(x_prompt, mod_p, od_p, om_p, diff_subln, uv_pairs, w_o_b, tm=tm_f)
    ffn_rows = bp * tp * FFN_SPLIT // bs
    assert bp * tp * FFN_SPLIT % bs == 0 and tp % ffn_rows == 0 and ffn_rows % 8 == 0

    xs = x_sample.reshape(1, ns, D_MODEL)
    tm_s = min(256, ns)
    rope_ps = _rope_tables(pos_s, MLA_ROPE, MLA_ROPE)
    sq, sk_f, _, sv_f, _, sckv_f, skpe_f, skcat, sqcat = _in_proj(
        xs, mod_s, _rope_tables(pos_s, DIFF_ROT, DIFF_DH), rope_ps, rope_ps,
        *shared, w_uq_p, uk_pairs, tm=tm_s, absorbed=True)
    n_hm = 2 * N_DIFF_HEADS
    sq5 = jnp.transpose(sq.reshape(bs, ts, n_hm, DIFF_DH), (0, 2, 1, 3))
    qd_blk = (sq5[:, :, :, None, :] * jnp.eye(n_hm, dtype=BF16)[None, :, None, :, None]
              ).reshape(bs, n_hm * ts, DQ_COLS)
    qm = jnp.transpose(sqcat.reshape(N_MLA_HEADS, bs, ts, QCAT), (1, 0, 2, 3)).reshape(bs, N_MLA_HEADS * ts, QCAT)
    kt_pages = jnp.transpose(cache_k_diff[0], (0, 2, 3, 4, 1)).reshape(n_pool, DQ_COLS, PAGE_SIZE)
    v_pages = cache_v_diff[0].reshape(n_pool, PAGE_SIZE * N_DIFF_HEADS, DIFF_VDIM)
    pe_pages = jnp.transpose(cache_kpe[0], (0, 2, 1))
    od_s, om_s, y_prompt = _decode_attn(page_table, lam, qd_blk, qm,
                                        sk_f.reshape(bs, ts, DQ_COLS), sv_f.reshape(bs, ts, DQ_COLS),
                                        skcat.reshape(bs, ts, QCAT), kt_pages, v_pages, cache_ckv[0], pe_pages,
                                        x1_p.reshape(bs // FFN_SPLIT, ffn_rows, D_MODEL), mod_p,
                                        ffn_norm, w_gu_b, w_down_b, final_norm2)
    y_prompt = y_prompt.reshape(bp, tp, D_MODEL)
    od_s = jnp.transpose(od_s, (0, 2, 1, 3)).reshape(1, ns, DIFF_WIDTH)
    om_s = jnp.transpose(om_s.reshape(bs, N_MLA_HEADS, ts, MLA_KV_RANK), (0, 2, 1, 3)
                         ).reshape(1, ns, N_MLA_HEADS * MLA_KV_RANK).astype(BF16)
    tm_fs = min(512, ns)
    x1_s = _out_proj(xs, mod_s, od_s, om_s, diff_subln, uv_pairs, w_o_b, tm=tm_fs)
    y_sample = _ffn(x1_s, mod_s, ffn_norm, w_gu_b, w_down_b, final_norm2, tm=tm_fs).reshape(bs, ts, D_MODEL)

    return (y_prompt, y_sample,
            dk_f.reshape(1, bp, tp, N_DIFF_HEADS, 2, DIFF_DH),
            dv_f.reshape(1, bp, tp, N_DIFF_HEADS, DIFF_VDIM),
            ckv_f.reshape(1, bp, tp, MLA_KV_RANK),
            kpe_f.reshape(1, bp, tp, MLA_ROPE),
            sk_f.reshape(1, bs, ts, N_DIFF_HEADS, 2, DIFF_DH),
            sv_f.reshape(1, bs, ts, N_DIFF_HEADS, DIFF_VDIM),
            sckv_f.reshape(1, bs, ts, MLA_KV_RANK),
            skpe_f.reshape(1, bs, ts, MLA_ROPE))
```

```python
import functools

import jax
import jax.numpy as jnp
import numpy as np
from jax import lax
from jax.experimental import pallas as pl
from jax.experimental.pallas import tpu as pltpu

F32 = jnp.float32
BF16 = jnp.bfloat16

D_MODEL = 1024
N_DIFF_HEADS = 4
DIFF_DH = 64
DIFF_VDIM = 2 * DIFF_DH
DIFF_ROT = DIFF_DH // 4
N_MLA_HEADS = 8
MLA_NOPE = 64
MLA_ROPE = 32
MLA_V = 64
MLA_Q_RANK = 384
MLA_KV_RANK = 256
DIFF_WIDTH = N_DIFF_HEADS * DIFF_VDIM
MLA_WIDTH = N_MLA_HEADS * MLA_V
DQ_COLS = N_DIFF_HEADS * 2 * DIFF_DH
IN_COLS = 3 * DQ_COLS + MLA_Q_RANK + MLA_KV_RANK + MLA_ROPE
D_FF = 2816
ROPE_THETA = 500000.0
NORM_EPS = 1e-6
PAGE_SIZE = 128
LOG2E = 1.4426950408889634
DIFF_SCALE = DIFF_DH ** -0.5 * LOG2E
MLA_SCALE = (MLA_NOPE + MLA_ROPE) ** -0.5 * LOG2E
LAMBDA_INIT = 0.8 - 0.6

LANES = 128
IN_COLS_PAD = 2304
QCAT = MLA_KV_RANK + LANES
PAGES_PER_CHUNK = 8
DECODE_SLOTS = 4
FFN_SPLIT = 4
VMEM_LIMIT = 56 * 1024 * 1024


def _nt_dot(a, b):
    return lax.dot_general(a, b, (((1,), (1,)), ((), ())), preferred_element_type=F32)


def _dot(a, b):
    return jnp.dot(a, b, preferred_element_type=F32)


def _rms(x, g):
    return x * lax.rsqrt(jnp.mean(x * x, axis=-1, keepdims=True) + NORM_EPS) * g


def _rope(x, tab_ref, half):
    return (x * tab_ref[0]
            + pltpu.roll(x, LANES - half, 1) * tab_ref[1]
            + pltpu.roll(x, half, 1) * tab_ref[2])


def _const_spec(shape):
    nd = len(shape)
    return pl.BlockSpec(shape, lambda *_: (0,) * nd, pipeline_mode=pl.Buffered(1))


def _mod_kernel(cp_ref, cs_ref, w_ref, b_ref, lv_ref, mp_ref, ms_ref, lam_ref):
    w = w_ref[...].astype(BF16)
    for c_ref, m_ref in ((cp_ref, mp_ref), (cs_ref, ms_ref)):
        c = c_ref[...]
        m_ref[...] = _dot((c * jax.nn.sigmoid(c)).astype(BF16), w) + b_ref[...]
    lv = lv_ref[...]
    s1 = jnp.sum(lv[0:1] * lv[1:2], axis=-1, keepdims=True)
    s2 = jnp.sum(lv[2:3] * lv[3:4], axis=-1, keepdims=True)
    lam_ref[...] = jnp.broadcast_to(jnp.exp(s1) - jnp.exp(s2) + LAMBDA_INIT, lam_ref.shape)


def _modulation(c_p, c_s, mod_w, mod_b, lam_vecs):
    n_p, n_s = c_p.shape[0], c_s.shape[0]
    tn = 1024
    return pl.pallas_call(
        _mod_kernel,
        grid=(6 * D_MODEL // tn,),
        in_specs=[pl.BlockSpec((n_p, D_MODEL), lambda j: (0, 0)),
                  pl.BlockSpec((n_s, D_MODEL), lambda j: (0, 0)),
                  pl.BlockSpec((D_MODEL, tn), lambda j: (0, j)),
                  pl.BlockSpec((1, tn), lambda j: (0, j)),
                  pl.BlockSpec((4, DIFF_DH), lambda j: (0, 0))],
        out_specs=[pl.BlockSpec((n_p, tn), lambda j: (0, j)),
                   pl.BlockSpec((n_s, tn), lambda j: (0, j)),
                   pl.BlockSpec((1, LANES), lambda j: (0, 0))],
        out_shape=[jax.ShapeDtypeStruct((n_p, 6 * D_MODEL), F32),
                   jax.ShapeDtypeStruct((n_s, 6 * D_MODEL), F32),
                   jax.ShapeDtypeStruct((1, LANES), F32)],
        compiler_params=pltpu.CompilerParams(dimension_semantics=("arbitrary",)),
        name="modulation",
    )(c_p, c_s, mod_w, mod_b, lam_vecs)


def _in_kernel(x_ref, sh_ref, sc_ref, rd_ref, rp_ref, rq_ref, an_ref, win_ref, qn_ref, kn_ref, wuq_ref, wk_ref,
               dq_ref, dkf_ref, dkb_ref, dvf_ref, dvb_ref, ckvf_ref, kpef_ref, *mla_refs, absorbed):
    x = x_ref[0]
    tm = x.shape[0]
    h = (_rms(x, an_ref[...]) * (1.0 + sc_ref[0]) + sh_ref[0]).astype(BF16)
    z = _dot(h, win_ref[...])
    for j in range(DQ_COLS // LANES):
        sl = slice(LANES * j, LANES * (j + 1))
        dq_ref[0, :, sl] = (_rope(z[:, sl], rd_ref, DIFF_ROT // 2) * DIFF_SCALE).astype(BF16)
        k = _rope(z[:, DQ_COLS + LANES * j:DQ_COLS + LANES * (j + 1)], rd_ref, DIFF_ROT // 2)
        dkf_ref[0, :, sl] = k
        dkb_ref[0, :, sl] = k.astype(BF16)
    dv = z[:, 2 * DQ_COLS:3 * DQ_COLS]
    for hh in range(N_DIFF_HEADS):
        dvf_ref[0, pl.ds(hh, tm, stride=N_DIFF_HEADS), :] = dv[:, DIFF_VDIM * hh:DIFF_VDIM * (hh + 1)]
    dvb_ref[0] = dv.astype(BF16)

    c0 = 3 * DQ_COLS
    c1 = c0 + MLA_Q_RANK
    cqn = _rms(z[:, c0:c0 + MLA_Q_RANK], qn_ref[...]).astype(BF16)
    ckvn = _rms(z[:, c1:c1 + MLA_KV_RANK], kn_ref[...])
    ckvf_ref[0] = ckvn
    kpe = _rope(z[:, c1 + MLA_KV_RANK:IN_COLS_PAD], rp_ref, MLA_ROPE // 2)
    kpef_ref[0] = kpe[:, :MLA_ROPE]
    q = _dot(cqn, wuq_ref[...])
    if not absorbed:
        qh_ref, kh_ref, vv_ref = mla_refs
        kv = _dot(ckvn.astype(BF16), wk_ref[...])
        kpe_at_rot = pltpu.roll(kpe, MLA_NOPE, 1)
        for hd in range(N_MLA_HEADS):
            sl = slice(LANES * hd, LANES * (hd + 1))
            qh_ref[0, :, sl] = (_rope(q[:, sl], rq_ref, MLA_ROPE // 2) * MLA_SCALE).astype(BF16)
            kh_ref[0, :, sl] = (kv[:, sl] + kpe_at_rot).astype(BF16)
        vv_ref[0] = kv[:, N_MLA_HEADS * LANES:].astype(BF16)
        return
    kcat_ref, qcat_ref = mla_refs
    nope_w = N_MLA_HEADS * MLA_NOPE
    qn = q[:, :nope_w].astype(BF16)
    for j in range(N_MLA_HEADS // 2):
        ql = _dot(qn[:, LANES * j:LANES * (j + 1)], wk_ref[j])
        qcat_ref[0, 2 * j, :, 0:MLA_KV_RANK] = (ql[:, :MLA_KV_RANK] * MLA_SCALE).astype(BF16)
        qcat_ref[0, 2 * j + 1, :, 0:MLA_KV_RANK] = (ql[:, MLA_KV_RANK:] * MLA_SCALE).astype(BF16)
    lane = lax.broadcasted_iota(jnp.int32, (tm, LANES), 1)
    heads_per_slab = LANES // MLA_ROPE
    for g in range(N_MLA_HEADS // heads_per_slab):
        qp = _rope(q[:, nope_w + LANES * g:nope_w + LANES * (g + 1)], rp_ref, MLA_ROPE // 2) * MLA_SCALE
        for i in range(heads_per_slab):
            moved = qp if i == 0 else pltpu.roll(qp, LANES - MLA_ROPE * i, 1)
            qcat_ref[0, heads_per_slab * g + i, :, MLA_KV_RANK:QCAT] = (
                jnp.where(lane < MLA_ROPE, moved, 0.0).astype(BF16))

    kcat_ref[0, :, 0:MLA_KV_RANK] = ckvn.astype(BF16)
    kcat_ref[0, :, MLA_KV_RANK:QCAT] = kpe.astype(BF16)


def _in_proj(x, mod, rope_d, rope_p, rope_q, attn_norm, w_in_p, q_norm, kv_norm, w_q, w_k, *, tm, absorbed):
    nb, t, _ = x.shape
    r = mod.shape[1]
    rb = 1 if r == 1 else tm
    mod_idx = (lambda col: (lambda b, i: (b, 0, col))) if r == 1 else (lambda col: (lambda b, i: (b, i, col)))
    tok = lambda w: pl.BlockSpec((1, tm, w), lambda b, i: (b, i, 0))
    rope_spec = pl.BlockSpec((3, tm, LANES), lambda b, i: (0, i, 0))
    out_shape = [jax.ShapeDtypeStruct((nb, t, DQ_COLS), BF16),
                 jax.ShapeDtypeStruct((nb, t, DQ_COLS), F32), jax.ShapeDtypeStruct((nb, t, DQ_COLS), BF16),
                 jax.ShapeDtypeStruct((nb, t * N_DIFF_HEADS, DIFF_VDIM), F32),
                 jax.ShapeDtypeStruct((nb, t, DQ_COLS), BF16),
                 jax.ShapeDtypeStruct((nb, t, MLA_KV_RANK), F32),
                 jax.ShapeDtypeStruct((nb, t, MLA_ROPE), F32)]
    out_specs = [tok(DQ_COLS), tok(DQ_COLS), tok(DQ_COLS),
                 pl.BlockSpec((1, tm * N_DIFF_HEADS, DIFF_VDIM), lambda b, i: (b, i, 0)), tok(DQ_COLS),
                 tok(MLA_KV_RANK), tok(MLA_ROPE)]
    if absorbed:
        out_shape += [jax.ShapeDtypeStruct((nb, t, QCAT), BF16),
                      jax.ShapeDtypeStruct((nb, N_MLA_HEADS, t, QCAT), BF16)]
        out_specs += [tok(QCAT), pl.BlockSpec((1, N_MLA_HEADS, tm, QCAT), lambda b, i: (b, 0, i, 0))]
    else:
        out_shape += [jax.ShapeDtypeStruct((nb, t, N_MLA_HEADS * LANES), BF16),
                      jax.ShapeDtypeStruct((nb, t, N_MLA_HEADS * LANES), BF16),
                      jax.ShapeDtypeStruct((nb, t, MLA_WIDTH), BF16)]
        out_specs += [tok(N_MLA_HEADS * LANES), tok(N_MLA_HEADS * LANES), tok(MLA_WIDTH)]
    return pl.pallas_call(
        functools.partial(_in_kernel, absorbed=absorbed),
        grid=(nb, t // tm),
        in_specs=[tok(D_MODEL),
                  pl.BlockSpec((1, rb, D_MODEL), mod_idx(0)),
                  pl.BlockSpec((1, rb, D_MODEL), mod_idx(1)),
                  rope_spec, rope_spec, rope_spec,
                  _const_spec((1, D_MODEL)),
                  _const_spec((D_MODEL, IN_COLS_PAD)),
                  _const_spec((1, MLA_Q_RANK)),
                  _const_spec((1, MLA_KV_RANK)),
                  _const_spec(w_q.shape),
                  _const_spec(w_k.shape)],
        out_specs=out_specs,
        out_shape=out_shape,
        compiler_params=pltpu.CompilerParams(dimension_semantics=("parallel", "parallel"),
                                             vmem_limit_bytes=VMEM_LIMIT),
        name="in_proj",
    )(x, mod, mod, rope_d, rope_p, rope_q, attn_norm, w_in_p, q_norm, kv_norm, w_q, w_k)


def _online_update(s, m_ref, l_ref, acc_ref, v, idx):
    blocks = [s[:, LANES * j:LANES * (j + 1)] for j in range(s.shape[1] // LANES)]
    m_old = m_ref[idx]
    m_new = jnp.maximum(m_old, jnp.max(functools.reduce(jnp.maximum, blocks), axis=-1, keepdims=True))
    alpha = jnp.exp2(m_old - m_new)
    p = [jnp.exp2(blk - m_new) for blk in blocks]
    l_ref[idx] = alpha * l_ref[idx] + functools.reduce(jnp.add, p)
    pv = _dot(jnp.concatenate(p, axis=1).astype(BF16), v)
    acc = acc_ref[idx]
    scale = jnp.concatenate([alpha] * (acc.shape[1] // LANES), axis=1)
    acc_ref[idx] = scale * acc + pv
    m_ref[idx] = m_new


def _prompt_attn_kernel(lam_ref, dq_ref, qh_ref, dk_ref, dv_ref, kh_ref, vv_ref, od_ref, om_ref,
                        qd_s, md_s, ld_s, accd_s, mm_s, lm_s, accm_s, *, tq):
    qi = pl.program_id(1)
    lane = lax.broadcasted_iota(jnp.int32, (tq, LANES), 1)
    for h in range(N_DIFF_HEADS):
        qh = dq_ref[0, :, LANES * h:LANES * (h + 1)]
        qd_s[h, 0:tq, :] = jnp.where(lane < DIFF_DH, qh, jnp.zeros_like(qh))
        qd_s[h, tq:2 * tq, :] = jnp.where(lane >= DIFF_DH, qh, jnp.zeros_like(qh))
    md_s[...] = jnp.full(md_s.shape, -jnp.inf, F32)
    ld_s[...] = jnp.zeros(ld_s.shape, F32)
    accd_s[...] = jnp.zeros(accd_s.shape, F32)
    mm_s[...] = jnp.full(mm_s.shape, -jnp.inf, F32)
    lm_s[...] = jnp.zeros(lm_s.shape, F32)
    accm_s[...] = jnp.zeros(accm_s.shape, F32)

    def key_tile(kt, diagonal):
        k0 = pl.multiple_of(kt * tq, tq)

        def causal(s):
            if not diagonal:
                return s
            r = lax.broadcasted_iota(jnp.int32, s.shape, 0) % tq
            c = lax.broadcasted_iota(jnp.int32, s.shape, 1)
            return jnp.where(c <= r, s, -jnp.inf)

        for h in range(N_DIFF_HEADS):
            sl = slice(LANES * h, LANES * (h + 1))
            s = causal(_nt_dot(qd_s[h], dk_ref[0, pl.ds(k0, tq), sl]))
            _online_update(s, md_s, ld_s, accd_s, dv_ref[0, pl.ds(k0, tq), sl], h)
        for hd in range(N_MLA_HEADS):
            sl = slice(LANES * hd, LANES * (hd + 1))
            s = causal(_nt_dot(qh_ref[0, :, sl], kh_ref[0, pl.ds(k0, tq), sl]))
            pair = slice(LANES * (hd // 2), LANES * (hd // 2 + 1))
            _online_update(s, mm_s, lm_s, accm_s, vv_ref[0, pl.ds(k0, tq), pair], hd)

    def body(kt, carry):
        key_tile(kt, False)
        return carry

    lax.fori_loop(0, qi, body, 0)
    key_tile(qi, True)

    lam = lam_ref[...]
    for h in range(N_DIFF_HEADS):
        o = accd_s[h] / jnp.sum(ld_s[h], axis=-1, keepdims=True)
        od_ref[0, :, LANES * h:LANES * (h + 1)] = o[:tq] - lam * o[tq:]
    for j in range(N_MLA_HEADS // 2):
        even = accm_s[2 * j] / jnp.sum(lm_s[2 * j], axis=-1, keepdims=True)
        odd = accm_s[2 * j + 1] / jnp.sum(lm_s[2 * j + 1], axis=-1, keepdims=True)
        om_ref[0, :, LANES * j:LANES * (j + 1)] = jnp.where(lane < MLA_V, even, odd).astype(BF16)


def _prompt_attn(lam, dq, qh, dk, dv, kh, vv, *, tq):
    b, t, _ = dq.shape
    kv = lambda w: pl.BlockSpec((1, t, w), lambda bi, qi: (bi, 0, 0))
    return pl.pallas_call(
        functools.partial(_prompt_attn_kernel, tq=tq),
        grid=(b, t // tq),
        in_specs=[pl.BlockSpec((1, LANES), lambda bi, qi: (0, 0)),
                  pl.BlockSpec((1, tq, DQ_COLS), lambda bi, qi: (bi, qi, 0)),
                  pl.BlockSpec((1, tq, N_MLA_HEADS * LANES), lambda bi, qi: (bi, qi, 0)),
                  kv(DQ_COLS), kv(DQ_COLS), kv(N_MLA_HEADS * LANES), kv(MLA_WIDTH)],
        out_specs=[pl.BlockSpec((1, tq, DIFF_WIDTH), lambda bi, qi: (bi, qi, 0)),
                   pl.BlockSpec((1, tq, MLA_WIDTH), lambda bi, qi: (bi, qi, 0))],
        out_shape=[jax.ShapeDtypeStruct((b, t, DIFF_WIDTH), F32),
                   jax.ShapeDtypeStruct((b, t, MLA_WIDTH), BF16)],
        scratch_shapes=[pltpu.VMEM((N_DIFF_HEADS, 2 * tq, LANES), BF16),
                        pltpu.VMEM((N_DIFF_HEADS, 2 * tq, LANES), F32),
                        pltpu.VMEM((N_DIFF_HEADS, 2 * tq, LANES), F32),
                        pltpu.VMEM((N_DIFF_HEADS, 2 * tq, DIFF_VDIM), F32),
                        pltpu.VMEM((N_MLA_HEADS, tq, LANES), F32),
                        pltpu.VMEM((N_MLA_HEADS, tq, LANES), F32),
                        pltpu.VMEM((N_MLA_HEADS, tq, LANES), F32)],
        compiler_params=pltpu.CompilerParams(dimension_semantics=("parallel", "arbitrary"),
                                             vmem_limit_bytes=VMEM_LIMIT),
        name="prompt_attn",
    )(lam, dq, qh, dk, dv, kh, vv)


def _decode_kernel(pt_ref, lam_ref, qd_ref, qm_ref, ks_ref, vs_ref, cs_ref,
                   kt_hbm, v_hbm, c_hbm, pe_hbm,
                   x_ref, sh_ref, sc_ref, g_ref, fn_ref, wgu_ref, wd_ref, final_ref,
                   od_ref, om_ref, y_ref,
                   kbuf, vbuf, cbuf, pbuf, sems, kpad, vpad, cpad, hbuf, facc, *, n_pages):
    ch = PAGES_PER_CHUNK
    n_chunks = n_pages // ch
    b = pl.program_id(0)
    nb = pl.num_programs(0)
    tok = ch * PAGE_SIZE
    n_rows = qd_ref.shape[1]
    heads_rows = n_rows // N_DIFF_HEADS

    def copies(bb, c, slot):
        out = []
        for j in range(ch):
            pg = pt_ref[bb, c * ch + j]
            out.append(pltpu.make_async_copy(kt_hbm.at[pg], kbuf.at[slot, :, pl.ds(j * PAGE_SIZE, PAGE_SIZE)],
                                             sems.at[0, slot]))
            out.append(pltpu.make_async_copy(v_hbm.at[pg], vbuf.at[slot, pl.ds(j * PAGE_SIZE * N_DIFF_HEADS,
                                                                              PAGE_SIZE * N_DIFF_HEADS), :],
                                             sems.at[1, slot]))
            out.append(pltpu.make_async_copy(c_hbm.at[pg], cbuf.at[slot, pl.ds(j * PAGE_SIZE, PAGE_SIZE), :],
                                             sems.at[2, slot]))
            out.append(pltpu.make_async_copy(pe_hbm.at[pg],
                                             pbuf.at[slot, pl.ds(0, MLA_ROPE), pl.ds(j * PAGE_SIZE, PAGE_SIZE)],
                                             sems.at[3, slot]))
        return out

    def start(cps):
        for i, cp in enumerate(cps):
            cp.start(priority=i % 2)

    @pl.when(b == 0)
    def _():
        pbuf[...] = jnp.zeros(pbuf.shape, F32)
        kpad[...] = jnp.zeros(kpad.shape, F32)
        vpad[...] = jnp.zeros(vpad.shape, F32)
        cpad[...] = jnp.zeros(cpad.shape, F32)
        for d in range(DECODE_SLOTS - 1):
            start(copies(d // n_chunks, d % n_chunks, d))

    part = lax.rem(b, FFN_SPLIT)
    n_ff = D_FF // FF_CHUNK

    @pl.when(part == 0)
    def _():
        hbuf[...] = (_rms(x_ref[0], fn_ref[...]) * (1.0 + sc_ref[0]) + sh_ref[0]).astype(BF16)

    for j in range(FFN_SPLIT):
        @pl.when(part == j)
        def _(j=j):
            acc = _ffn_partial(hbuf[...], wgu_ref, wd_ref, range(n_ff * j // FFN_SPLIT, n_ff * (j + 1) // FFN_SPLIT))
            if j > 0:
                acc = acc + facc[...]
            if j < FFN_SPLIT - 1:
                facc[...] = acc
            else:
                y_ref[0] = _rms(x_ref[0] + g_ref[0] * acc, final_ref[...])

    qd = qd_ref[0]
    qm = qm_ref[0]
    qm_lat = qm[:, :MLA_KV_RANK]
    qm_pe = qm[:, MLA_KV_RANK:]

    def softmax_step(s, m, l):
        m_new = jnp.maximum(m, jnp.max(s, axis=-1, keepdims=True))
        alpha = jnp.exp2(m - m_new)
        p = jnp.exp2(s - m_new)
        return p.astype(BF16), alpha, m_new, alpha * l + jnp.sum(p, axis=-1, keepdims=True)

    def chunk_body(c, carry):
        md, ld, accd, mm, lm, accm = carry
        g = b * n_chunks + c
        slot = lax.rem(g, DECODE_SLOTS)
        ahead = g + (DECODE_SLOTS - 1)

        @pl.when(ahead < nb * n_chunks)
        def _():
            start(copies(lax.div(ahead, n_chunks), lax.rem(ahead, n_chunks), lax.rem(ahead, DECODE_SLOTS)))

        for cp in copies(b, c, slot):
            cp.wait()

        s_d = _dot(qd, kbuf[slot].astype(BF16))
        cb = cbuf[slot].astype(BF16)
        s_m = _nt_dot(qm_lat, cb) + _dot(qm_pe, pbuf[slot].astype(BF16))
        p_d, a_d, md, ld = softmax_step(s_d, md, ld)
        p_m, a_m, mm, lm = softmax_step(s_m, mm, lm)
        new_accd = []
        for h in range(N_DIFF_HEADS):
            rows = slice(heads_rows * h, heads_rows * (h + 1))
            vh = vbuf[slot, pl.ds(h, tok, stride=N_DIFF_HEADS), :].astype(BF16)
            new_accd.append(a_d[rows] * accd[h] + _dot(p_d, vh)[rows])
        accm = a_m * accm + _dot(p_m, cb)
        return md, ld, tuple(new_accd), mm, lm, accm

    init = (jnp.full((n_rows, 1), -jnp.inf, F32), jnp.zeros((n_rows, 1), F32),
            tuple(jnp.zeros((heads_rows, DIFF_VDIM), F32) for _ in range(N_DIFF_HEADS)),
            jnp.full((n_rows, 1), -jnp.inf, F32), jnp.zeros((n_rows, 1), F32),
            jnp.zeros((n_rows, MLA_KV_RANK), F32))
    md, ld, accd, mm, lm, accm = lax.fori_loop(0, n_chunks, chunk_body, init)

    n_new = ks_ref.shape[1]
    kpad[0:n_new, :] = ks_ref[0]
    vpad[0:n_new, :] = vs_ref[0]
    cpad[0:n_new, :] = cs_ref[0].astype(F32)
    qpos = lax.broadcasted_iota(jnp.int32, (n_rows, PAGE_SIZE), 0) % n_new
    kpos = lax.broadcasted_iota(jnp.int32, (n_rows, PAGE_SIZE), 1)
    ok = kpos <= qpos
    cself = cpad[...].astype(BF16)
    s_d = jnp.where(ok, _nt_dot(qd, kpad[...].astype(BF16)), -jnp.inf)
    s_m = jnp.where(ok, _nt_dot(qm, cself), -jnp.inf)
    p_d, a_d, md, ld = softmax_step(s_d, md, ld)
    p_m, a_m, mm, lm = softmax_step(s_m, mm, lm)
    pv = _dot(p_d, vpad[...].astype(BF16))
    accm = a_m * accm + _dot(p_m, cself[:, :MLA_KV_RANK])

    lam = lam_ref[...]
    half = heads_rows // 2
    for h in range(N_DIFF_HEADS):
        rows = slice(heads_rows * h, heads_rows * (h + 1))
        acc = a_d[rows] * accd[h] + pv[rows, DIFF_VDIM * h:DIFF_VDIM * (h + 1)]
        o = acc / ld[rows]
        od_ref[0, h] = o[:half] - lam * o[half:]
    om_ref[0] = accm / lm


def _decode_attn(page_table, lam, qd_blk, qm, k_self, v_self, kc_self, kt_pages, v_pages, c_pages, pe_pages,
                 x_ffn, mod_ffn, ffn_norm, w_gu, w_down, final_norm):
    nb, n_pages = page_table.shape
    assert x_ffn.shape[0] * FFN_SPLIT == nb and nb % mod_ffn.shape[0] == 0
    steps_per_group = nb // mod_ffn.shape[0]
    ffn_rows = x_ffn.shape[1]
    ffn_tile = pl.BlockSpec((1, ffn_rows, D_MODEL), lambda b, pt: (b // FFN_SPLIT, 0, 0))
    ffn_mod = lambda col: pl.BlockSpec((1, 1, D_MODEL), lambda b, pt: (b // steps_per_group, 0, col))
    n_rows = qd_blk.shape[1]
    n_new = k_self.shape[1]
    ch = PAGES_PER_CHUNK
    tok = ch * PAGE_SIZE
    row_blk = lambda w: pl.BlockSpec((1, n_rows, w), lambda b, pt: (b, 0, 0))
    new_blk = lambda w: pl.BlockSpec((1, n_new, w), lambda b, pt: (b, 0, 0))
    any_spec = pl.BlockSpec(memory_space=pl.ANY)
    grid_spec = pltpu.PrefetchScalarGridSpec(
        num_scalar_prefetch=1,
        grid=(nb,),
        in_specs=[pl.BlockSpec((1, LANES), lambda b, pt: (0, 0)),
                  row_blk(DQ_COLS), row_blk(QCAT), new_blk(DQ_COLS), new_blk(DQ_COLS), new_blk(QCAT),
                  any_spec, any_spec, any_spec, any_spec,
                  ffn_tile, ffn_mod(3), ffn_mod(4), ffn_mod(5),
                  _const_spec((1, D_MODEL)), _const_spec((D_MODEL, 2 * D_FF)), _const_spec((D_FF, D_MODEL)),
                  _const_spec((1, D_MODEL))],
        out_specs=[pl.BlockSpec((1, N_DIFF_HEADS, n_new, DIFF_VDIM), lambda b, pt: (b, 0, 0, 0)),
                   pl.BlockSpec((1, n_rows, MLA_KV_RANK), lambda b, pt: (b, 0, 0)),
                   ffn_tile],
        scratch_shapes=[pltpu.VMEM((DECODE_SLOTS, DQ_COLS, tok), F32),
                        pltpu.VMEM((DECODE_SLOTS, tok * N_DIFF_HEADS, DIFF_VDIM), F32),
                        pltpu.VMEM((DECODE_SLOTS, tok, MLA_KV_RANK), F32),
                        pltpu.VMEM((DECODE_SLOTS, LANES, tok), F32),
                        pltpu.SemaphoreType.DMA((4, DECODE_SLOTS)),
                        pltpu.VMEM((PAGE_SIZE, DQ_COLS), F32),
                        pltpu.VMEM((PAGE_SIZE, DQ_COLS), F32),
                        pltpu.VMEM((PAGE_SIZE, QCAT), F32),
                        pltpu.VMEM((ffn_rows, D_MODEL), BF16),
                        pltpu.VMEM((ffn_rows, D_MODEL), F32)])
    return pl.pallas_call(
        functools.partial(_decode_kernel, n_pages=n_pages),
        grid_spec=grid_spec,
        out_shape=[jax.ShapeDtypeStruct((nb, N_DIFF_HEADS, n_new, DIFF_VDIM), F32),
                   jax.ShapeDtypeStruct((nb, n_rows, MLA_KV_RANK), F32),
                   jax.ShapeDtypeStruct(x_ffn.shape, F32)],
        compiler_params=pltpu.CompilerParams(dimension_semantics=("arbitrary",),
                                             vmem_limit_bytes=VMEM_LIMIT),
        name="decode_attn",
    )(page_table, lam, qd_blk, qm, k_self, v_self, kc_self, kt_pages, v_pages, c_pages, pe_pages,
      x_ffn, mod_ffn, mod_ffn, mod_ffn, ffn_norm, w_gu, w_down, final_norm)


def _out_kernel(x_ref, g_ref, od_ref, om_ref, sub_ref, wuv_ref, wo_ref, o_ref):
    parts = []
    for h in range(N_DIFF_HEADS):
        o = od_ref[0, :, DIFF_VDIM * h:DIFF_VDIM * (h + 1)]
        parts.append((_rms(o, sub_ref[...]) * (1.0 - LAMBDA_INIT)).astype(BF16))
    if om_ref.shape[-1] == MLA_WIDTH:
        parts.append(om_ref[0])
    else:
        pair = 2 * MLA_KV_RANK
        for j in range(N_MLA_HEADS // 2):
            parts.append(_dot(om_ref[0, :, pair * j:pair * (j + 1)], wuv_ref[j]).astype(BF16))
    mixed = jnp.concatenate(parts, axis=1)
    o_ref[0] = x_ref[0] + g_ref[0] * _dot(mixed, wo_ref[...])


def _out_proj(x, mod, od, om, diff_subln, w_uv_pairs, w_o, *, tm):
    nb, t, _ = x.shape
    r = mod.shape[1]
    rb = 1 if r == 1 else tm
    gate_idx = (lambda b, i: (b, 0, 2)) if r == 1 else (lambda b, i: (b, i, 2))
    tok = lambda w: pl.BlockSpec((1, tm, w), lambda b, i: (b, i, 0))
    return pl.pallas_call(
        _out_kernel,
        grid=(nb, t // tm),
        in_specs=[tok(D_MODEL), pl.BlockSpec((1, rb, D_MODEL), gate_idx),
                  tok(DIFF_WIDTH), tok(om.shape[-1]),
                  _const_spec((1, DIFF_VDIM)),
                  _const_spec((N_MLA_HEADS // 2, 2 * MLA_KV_RANK, LANES)),
                  _const_spec((DIFF_WIDTH + MLA_WIDTH, D_MODEL))],
        out_specs=tok(D_MODEL),
        out_shape=jax.ShapeDtypeStruct((nb, t, D_MODEL), F32),
        compiler_params=pltpu.CompilerParams(dimension_semantics=("parallel", "parallel"),
                                             vmem_limit_bytes=VMEM_LIMIT),
        name="out_proj",
    )(x, mod, od, om, diff_subln, w_uv_pairs, w_o)


FF_CHUNK = 256


def _ffn_partial(h, wgu_ref, wd_ref, chunks):
    acc = None
    for c in chunks:
        lo = FF_CHUNK * c
        gate = _dot(h, wgu_ref[:, lo:lo + FF_CHUNK])
        up = _dot(h, wgu_ref[:, D_FF + lo:D_FF + lo + FF_CHUNK])
        act = (gate * jax.nn.sigmoid(gate) * up).astype(BF16)
        part = _dot(act, wd_ref[lo:lo + FF_CHUNK, :])
        acc = part if acc is None else acc + part
    return acc


def _ffn_tile(x, sh, sc, g, ffn_norm, wgu_ref, wd_ref, final_norm):
    h = (_rms(x, ffn_norm) * (1.0 + sc) + sh).astype(BF16)
    acc = _ffn_partial(h, wgu_ref, wd_ref, range(D_FF // FF_CHUNK))
    return _rms(x + g * acc, final_norm)


def _ffn_kernel(x_ref, sh_ref, sc_ref, g_ref, fn_ref, wgu_ref, wd_ref, final_ref, y_ref):
    y_ref[0] = _ffn_tile(x_ref[0], sh_ref[0], sc_ref[0], g_ref[0], fn_ref[...], wgu_ref, wd_ref, final_ref[...])


def _ffn(x, mod, ffn_norm, w_gu, w_down, final_norm, *, tm):
    nb, t, _ = x.shape
    r = mod.shape[1]
    rb = 1 if r == 1 else tm
    mod_idx = (lambda col: (lambda b, i: (b, 0, col))) if r == 1 else (lambda col: (lambda b, i: (b, i, col)))
    tok = pl.BlockSpec((1, tm, D_MODEL), lambda b, i: (b, i, 0))
    return pl.pallas_call(
        _ffn_kernel,
        grid=(nb, t // tm),
        in_specs=[tok,
                  pl.BlockSpec((1, rb, D_MODEL), mod_idx(3)),
                  pl.BlockSpec((1, rb, D_MODEL), mod_idx(4)),
                  pl.BlockSpec((1, rb, D_MODEL), mod_idx(5)),
                  _const_spec((1, D_MODEL)),
                  _const_spec((D_MODEL, 2 * D_FF)),
                  _const_spec((D_FF, D_MODEL)),
                  _const_spec((1, D_MODEL))],
        out_specs=tok,
        out_shape=jax.ShapeDtypeStruct((nb, t, D_MODEL), F32),
        compiler_params=pltpu.CompilerParams(dimension_semantics=("parallel", "parallel"),
                                             vmem_limit_bytes=VMEM_LIMIT),
        name="ffn",
    )(x, mod, mod, mod, ffn_norm, w_gu, w_down, final_norm)


def _rope_tables(pos, rot, period, offset=0):
    half = rot // 2
    j = (np.arange(LANES) - offset) % period
    first = j < half
    second = (j >= half) & (j < rot)
    freq_idx = np.where(first, j, np.clip(j - half, 0, half - 1)).astype(np.float32)
    inv = ROPE_THETA ** (-(jnp.asarray(freq_idx) * 2.0 / rot))
    ang = pos.astype(F32)[:, None] * inv[None, :]
    cos, sin = jnp.cos(ang), jnp.sin(ang)
    return jnp.stack([jnp.where(first | second, cos, 1.0),
                      jnp.where(first, -sin, 0.0),
                      jnp.where(second, sin, 0.0)])


def _prep_weights(w_in, w_uq, w_ukv, w_o, w_gate_up, w_down):
    w_in_p = jnp.pad(w_in, ((0, 0), (0, IN_COLS_PAD - IN_COLS))).astype(BF16)
    uq = w_uq.reshape(MLA_Q_RANK, N_MLA_HEADS, MLA_NOPE + MLA_ROPE)
    w_uq_p = jnp.concatenate([uq[:, :, :MLA_NOPE].reshape(MLA_Q_RANK, -1),
                              uq[:, :, MLA_NOPE:].reshape(MLA_Q_RANK, -1)], axis=1).astype(BF16)
    ukv = w_ukv.reshape(MLA_KV_RANK, N_MLA_HEADS, MLA_NOPE + MLA_V)
    uk_t = jnp.transpose(ukv[:, :, :MLA_NOPE], (1, 2, 0))
    uv = jnp.transpose(ukv[:, :, MLA_NOPE:], (1, 0, 2))
    zk = jnp.zeros_like(uk_t[0])
    zv = jnp.zeros_like(uv[0])
    uk_pairs = jnp.stack([jnp.block([[uk_t[2 * j], zk], [zk, uk_t[2 * j + 1]]])
                          for j in range(N_MLA_HEADS // 2)]).astype(BF16)
    uv_pairs = jnp.stack([jnp.block([[uv[2 * j], zv], [zv, uv[2 * j + 1]]])
                          for j in range(N_MLA_HEADS // 2)]).astype(BF16)
    pad_q = jnp.zeros((MLA_Q_RANK, N_MLA_HEADS, LANES - MLA_NOPE - MLA_ROPE), w_uq.dtype)
    w_uq_h = jnp.concatenate([uq, pad_q], axis=2).reshape(MLA_Q_RANK, N_MLA_HEADS * LANES).astype(BF16)
    pad_k = jnp.zeros((MLA_KV_RANK, N_MLA_HEADS, LANES - MLA_NOPE), w_ukv.dtype)
    w_kv = jnp.concatenate([jnp.concatenate([ukv[:, :, :MLA_NOPE], pad_k], axis=2).reshape(MLA_KV_RANK, -1),
                            ukv[:, :, MLA_NOPE:].reshape(MLA_KV_RANK, -1)], axis=1).astype(BF16)
    return (w_in_p, w_uq_p, uk_pairs, uv_pairs, w_uq_h, w_kv,
            w_o.astype(BF16), w_gate_up.astype(BF16), w_down.astype(BF16))


def kernel(x_prompt, x_sample, c_prompt, c_sample, cache_k_diff, cache_v_diff, cache_ckv, cache_kpe, page_table, mod_w, mod_b, attn_norm, w_in, q_norm, kv_norm, w_uq, w_ukv, lambda_q1, lambda_k1, lambda_q2, lambda_k2, diff_subln, w_o, ffn_norm, w_gate_up, w_down, final_norm):
    assert mod_w.shape[0] == 1, "single-layer trunk"
    bp, tp, _ = x_prompt.shape
    bs, ts, _ = x_sample.shape
    n_pool = cache_ckv.shape[1]
    n_pages = page_table.shape[1]
    assert n_pages % PAGES_PER_CHUNK == 0 and bs * (n_pages // PAGES_PER_CHUNK) >= DECODE_SLOTS - 1
    ns = bs * ts

    w_in_p, w_uq_p, uk_pairs, uv_pairs, w_uq_h, w_kv, w_o_b, w_gu_b, w_down_b = _prep_weights(
        w_in[0], w_uq[0], w_ukv[0], w_o[0], w_gate_up[0], w_down[0])
    final_norm2 = final_norm.reshape(1, D_MODEL)

    lam_vecs = jnp.concatenate([lambda_q1, lambda_k1, lambda_q2, lambda_k2], axis=0)
    mod_p, mod_s, lam = _modulation(c_prompt, jnp.repeat(c_sample, ts, axis=0), mod_w[0], mod_b, lam_vecs)
    mod_p = mod_p.reshape(bp, 1, 6 * D_MODEL)
    mod_s = mod_s.reshape(1, ns, 6 * D_MODEL)

    pos_p = jnp.arange(tp)
    pos_s = n_pages * PAGE_SIZE + (jnp.arange(ns) % ts)
    shared = (attn_norm, w_in_p, q_norm, kv_norm)

    tm_p = min(256, tp)
    dq, dk_f, dk_b, dv_f, dv_b, ckv_f, kpe_f, qh, kh, vv = _in_proj(
        x_prompt, mod_p, _rope_tables(pos_p, DIFF_ROT, DIFF_DH), _rope_tables(pos_p, MLA_ROPE, MLA_ROPE),
        _rope_tables(pos_p, MLA_ROPE, LANES, offset=MLA_NOPE), *shared, w_uq_h, w_kv, tm=min(512, tp),
        absorbed=False)
    od_p, om_p = _prompt_attn(lam, dq, qh, dk_b, dv_b, kh, vv, tq=tm_p)
    tm_f = min(512, tp)
    x1_p = _out_proj(x_prompt, mod_p, od_p, om_p, diff_subln, uv_pairs, w_o_b, tm=tm_f)
    ffn_rows = bp * tp * FFN_SPLIT // bs
    assert bp * tp * FFN_SPLIT % bs == 0 and tp % ffn_rows == 0 and ffn_rows % 8 == 0

    xs = x_sample.reshape(1, ns, D_MODEL)
    tm_s = min(256, ns)
    rope_ps = _rope_tables(pos_s, MLA_ROPE, MLA_ROPE)
    sq, sk_f, _, sv_f, _, sckv_f, skpe_f, skcat, sqcat = _in_proj(
        xs, mod_s, _rope_tables(pos_s, DIFF_ROT, DIFF_DH), rope_ps, rope_ps,
        *shared, w_uq_p, uk_pairs, tm=tm_s, absorbed=True)
    n_hm = 2 * N_DIFF_HEADS
    sq5 = jnp.transpose(sq.reshape(bs, ts, n_hm, DIFF_DH), (0, 2, 1, 3))
    qd_blk = (sq5[:, :, :, None, :] * jnp.eye(n_hm, dtype=BF16)[None, :, None, :, None]
              ).reshape(bs, n_hm * ts, DQ_COLS)
    qm = jnp.transpose(sqcat.reshape(N_MLA_HEADS, bs, ts, QCAT), (1, 0, 2, 3)).reshape(bs, N_MLA_HEADS * ts, QCAT)
    kt_pages = jnp.transpose(cache_k_diff[0], (0, 2, 3, 4, 1)).reshape(n_pool, DQ_COLS, PAGE_SIZE)
    v_pages = cache_v_diff[0].reshape(n_pool, PAGE_SIZE * N_DIFF_HEADS, DIFF_VDIM)
    pe_pages = jnp.transpose(cache_kpe[0], (0, 2, 1))
    od_s, om_s, y_prompt = _decode_attn(page_table, lam, qd_blk, qm,
                                        sk_f.reshape(bs, ts, DQ_COLS), sv_f.reshape(bs, ts, DQ_COLS),
                                        skcat.reshape(bs, ts, QCAT), kt_pages, v_pages, cache_ckv[0], pe_pages,
                                        x1_p.reshape(bs // FFN_SPLIT, ffn_rows, D_MODEL), mod_p,
                                        ffn_norm, w_gu_b, w_down_b, final_norm2)
    y_prompt = y_prompt.reshape(bp, tp, D_MODEL)
    od_s = jnp.transpose(od_s, (0, 2, 1, 3)).reshape(1, ns, DIFF_WIDTH)
    om_s = jnp.transpose(om_s.reshape(bs, N_MLA_HEADS, ts, MLA_KV_RANK), (0, 2, 1, 3)
                         ).reshape(1, ns, N_MLA_HEADS * MLA_KV_RANK).astype(BF16)
    tm_fs = min(512, ns)
    x1_s = _out_proj(xs, mod_s, od_s, om_s, diff_subln, uv_pairs, w_o_b, tm=tm_fs)
    y_sample = _ffn(x1_s, mod_s, ffn_norm, w_gu_b, w_down_b, final_norm2, tm=tm_fs).reshape(bs, ts, D_MODEL)

    return (y_prompt, y_sample,
            dk_f.reshape(1, bp, tp, N_DIFF_HEADS, 2, DIFF_DH),
            dv_f.reshape(1, bp, tp, N_DIFF_HEADS, DIFF_VDIM),
            ckv_f.reshape(1, bp, tp, MLA_KV_RANK),
            kpe_f.reshape(1, bp, tp, MLA_ROPE),
            sk_f.reshape(1, bs, ts, N_DIFF_HEADS, 2, DIFF_DH),
            sv_f.reshape(1, bs, ts, N_DIFF_HEADS, DIFF_VDIM),
            sckv_f.reshape(1, bs, ts, MLA_KV_RANK),
            skpe_f.reshape(1, bs, ts, MLA_ROPE))
```
